```python
import jax, jax.numpy as jnp
from jax import lax
import numpy as np

D_MODEL = 1024
BATCH = 4
SEQ = 4096
DEPTH = 2

CHUNK = 64
D_RNN = 1280
RNN_HEADS = 20
RNN_HEAD_DIM = D_RNN // RNN_HEADS
CONV_WIDTH = 4
LRU_C = 8.0
D_SGU = 1024
SGU_GROUPS = 8
SGU_GROUP_DIM = D_SGU // SGU_GROUPS
SGU_BLOCK = 128
N_BRANCH = 2
D_FF = 4 * D_MODEL
D_IN = 2 * D_RNN + 2 * D_SGU + N_BRANCH * D_MODEL
EPS = 1e-6

kernel_name = "hybrid_rglru_sgu_gated_encoder"


def rmsnorm(x, g):
    xf = x.astype(jnp.float32)
    y = xf * lax.rsqrt(jnp.mean(xf * xf, axis=-1, keepdims=True) + EPS)
    return (y * g.astype(jnp.float32)).astype(x.dtype)


def layernorm(x, g, b):
    xf = x.astype(jnp.float32)
    mu = jnp.mean(xf, axis=-1, keepdims=True)
    xc = xf - mu
    y = xc * lax.rsqrt(jnp.mean(xc * xc, axis=-1, keepdims=True) + EPS)
    return (y * g.astype(jnp.float32) + b.astype(jnp.float32)).astype(x.dtype)


def causal_depthwise_conv(x, w, b):
    s = x.shape[1]
    xp = jnp.pad(x, ((0, 0), (CONV_WIDTH - 1, 0), (0, 0)))
    y = b
    for k in range(CONV_WIDTH):
        y = y + xp[:, k:k + s, :] * w[k]
    return y


def rg_lru(x, w_a, b_a, w_x, b_x, lam):
    bsz, s, _ = x.shape
    xh = x.reshape(bsz, s, RNN_HEADS, RNN_HEAD_DIM)
    r = jax.nn.sigmoid(jnp.einsum('bshi,hij->bshj', xh, w_a) + b_a).reshape(bsz, s, D_RNN)
    i = jax.nn.sigmoid(jnp.einsum('bshi,hij->bshj', xh, w_x) + b_x).reshape(bsz, s, D_RNN)
    log_a = (-LRU_C * r.astype(jnp.float32)) * jax.nn.softplus(-lam.astype(jnp.float32))
    a = jnp.exp(log_a)
    norm = jnp.sqrt(-jnp.expm1(2.0 * log_a))
    u = norm * (i * x).astype(jnp.float32)

    def combine(left, right):
        a_l, b_l = left
        a_r, b_r = right
        return a_l * a_r, a_r * b_l + b_r

    _, h = lax.associative_scan(combine, (a, u), axis=1)
    return h.astype(x.dtype)


def spatial_gating(u, v, w_s, b_s, ln_g, ln_b):
    bsz, s, _ = u.shape
    nblk = s // SGU_BLOCK
    v = layernorm(v, ln_g, ln_b)
    vb = v.reshape(bsz, nblk, SGU_BLOCK, SGU_GROUPS, SGU_GROUP_DIM)
    chunk_id = jnp.arange(SGU_BLOCK) // CHUNK
    mask = (chunk_id[:, None] >= chunk_id[None, :]).astype(w_s.dtype)
    mixed = jnp.einsum('gts,bnsgc->bntgc', w_s * mask, vb)
    mixed = mixed + jnp.transpose(b_s)[None, None, :, :, None]
    return u * mixed.reshape(bsz, s, D_SGU)


def hybrid_layer(x, norm_mix_g, w_in, conv_w, conv_b, lru_w_a, lru_b_a, lru_w_x, lru_b_x,
                 lru_lambda, sgu_ln_g, sgu_ln_b, sgu_w_s, sgu_b_s, w_branch_a, w_branch_b,
                 w_out, norm_ffn_g, w_up, w_down):
    h = rmsnorm(x, norm_mix_g)
    proj = jnp.einsum('bsd,de->bse', h, w_in)
    cuts = np.cumsum([D_RNN, D_RNN, D_SGU, D_SGU, D_MODEL])
    x_rnn, g_rnn, u, v, gate_a, gate_b = jnp.split(proj, cuts, axis=-1)

    xr = causal_depthwise_conv(x_rnn, conv_w, conv_b)
    ya = rg_lru(xr, lru_w_a, lru_b_a, lru_w_x, lru_b_x, lru_lambda) * jax.nn.gelu(g_rnn)
    ya = jnp.einsum('bsr,rd->bsd', ya, w_branch_a)

    yb = spatial_gating(jax.nn.gelu(u), jax.nn.gelu(v), sgu_w_s, sgu_b_s, sgu_ln_g, sgu_ln_b)
    yb = jnp.einsum('bsc,cd->bsd', yb, w_branch_b)

    merged = jax.nn.sigmoid(gate_a) * ya + jax.nn.sigmoid(gate_b) * yb
    x = x + jnp.einsum('bsd,de->bse', merged, w_out)

    h2 = rmsnorm(x, norm_ffn_g)
    f = jnp.square(jax.nn.relu(jnp.einsum('bsd,df->bsf', h2, w_up)))
    return x + jnp.einsum('bsf,fd->bsd', f, w_down)


def setup_inputs(seed: int = 0) -> dict:
    key = jax.random.key(seed)
    ks = jax.random.split(key, 24)
    f32 = jnp.float32

    def nrm(k, shape, scale):
        return jax.random.normal(k, shape, f32) * scale

    a_c = jax.random.uniform(ks[7], (DEPTH, D_RNN), f32, 0.9, 0.999)
    a0 = a_c ** (1.0 / LRU_C)
    lru_lambda = jnp.log(a0) - jnp.log1p(-a0)

    return {
        "x": nrm(ks[0], (BATCH, SEQ, D_MODEL), 1.0),
        "norm_mix_g": 1.0 + nrm(ks[1], (DEPTH, D_MODEL), 0.02),
        "w_in": nrm(ks[2], (DEPTH, D_MODEL, D_IN), D_MODEL ** -0.5),
        "conv_w": nrm(ks[3], (DEPTH, CONV_WIDTH, D_RNN), CONV_WIDTH ** -0.5),
        "conv_b": nrm(ks[4], (DEPTH, D_RNN), 0.02),
        "lru_w_a": nrm(ks[5], (DEPTH, RNN_HEADS, RNN_HEAD_DIM, RNN_HEAD_DIM), RNN_HEAD_DIM ** -0.5),
        "lru_b_a": nrm(ks[6], (DEPTH, RNN_HEADS, RNN_HEAD_DIM), 0.02),
        "lru_w_x": nrm(ks[8], (DEPTH, RNN_HEADS, RNN_HEAD_DIM, RNN_HEAD_DIM), RNN_HEAD_DIM ** -0.5),
        "lru_b_x": nrm(ks[9], (DEPTH, RNN_HEADS, RNN_HEAD_DIM), 0.02),
        "lru_lambda": lru_lambda,
        "sgu_ln_g": 1.0 + nrm(ks[10], (DEPTH, D_SGU), 0.02),
        "sgu_ln_b": nrm(ks[11], (DEPTH, D_SGU), 0.02),
        "sgu_w_s": nrm(ks[12], (DEPTH, SGU_GROUPS, SGU_BLOCK, SGU_BLOCK), SGU_BLOCK ** -0.5),
        "sgu_b_s": 1.0 + nrm(ks[13], (DEPTH, SGU_GROUPS, SGU_BLOCK), 0.02),
        "w_branch_a": nrm(ks[14], (DEPTH, D_RNN, D_MODEL), D_RNN ** -0.5),
        "w_branch_b": nrm(ks[15], (DEPTH, D_SGU, D_MODEL), D_SGU ** -0.5),
        "w_out": nrm(ks[16], (DEPTH, D_MODEL, D_MODEL), D_MODEL ** -0.5),
        "norm_ffn_g": 1.0 + nrm(ks[17], (DEPTH, D_MODEL), 0.02),
        "w_up": nrm(ks[18], (DEPTH, D_MODEL, D_FF), D_MODEL ** -0.5),
        "w_down": nrm(ks[19], (DEPTH, D_FF, D_MODEL), D_FF ** -0.5),
        "final_norm_g": 1.0 + nrm(ks[20], (D_MODEL,), 0.02),
    }


def reference(x, norm_mix_g, w_in, conv_w, conv_b, lru_w_a, lru_b_a, lru_w_x, lru_b_x,
              lru_lambda, sgu_ln_g, sgu_ln_b, sgu_w_s, sgu_b_s, w_branch_a, w_branch_b,
              w_out, norm_ffn_g, w_up, w_down, final_norm_g):
    for l in range(DEPTH):
        x = hybrid_layer(x, norm_mix_g[l], w_in[l], conv_w[l], conv_b[l], lru_w_a[l], lru_b_a[l],
                         lru_w_x[l], lru_b_x[l], lru_lambda[l], sgu_ln_g[l], sgu_ln_b[l],
                         sgu_w_s[l], sgu_b_s[l], w_branch_a[l], w_branch_b[l], w_out[l],
                         norm_ffn_g[l], w_up[l], w_down[l])
    return rmsnorm(x, final_norm_g)
```

```python
import functools

import jax
import jax.numpy as jnp
from jax import lax
from jax.experimental import pallas as pl
from jax.experimental.pallas import tpu as pltpu

D_MODEL = 1024
D_RNN = 1280
RNN_HEADS = 20
RNN_HEAD_DIM = D_RNN // RNN_HEADS
CONV_WIDTH = 4
LRU_C = 8.0
D_SGU = 1024
SGU_GROUPS = 8
SGU_GROUP_DIM = D_SGU // SGU_GROUPS
SGU_BLOCK = 128
CHUNK = 64
D_FF = 4 * D_MODEL
EPS = 1e-6

OFF_XRNN = 0
OFF_GRNN = OFF_XRNN + D_RNN
OFF_U = OFF_GRNN + D_RNN
OFF_V = OFF_U + D_SGU
OFF_GATE_A = OFF_V + D_SGU
OFF_GATE_B = OFF_GATE_A + D_MODEL
D_IN = OFF_GATE_B + D_MODEL

SUBLANES = 8
GATE_BLOCK = 256
N_GATE_BLOCKS = D_RNN // GATE_BLOCK
HEADS_PER_GATE_BLOCK = GATE_BLOCK // RNN_HEAD_DIM
FF_CHUNK = 1024

MIXER_TILE = 256
FFN_TILE = 512
VMEM_LIMIT_BYTES = 56 * 1024 * 1024

F32 = jnp.float32
BF16 = jnp.bfloat16


def _dot(a, b):
    return jnp.dot(a, b, preferred_element_type=F32)


def _rmsnorm(x, g):
    return x * lax.rsqrt(jnp.mean(x * x, axis=-1, keepdims=True) + EPS) * g


def _mixer_kernel(x_ref, ng_ref, win_ref, cw_ref, cb_ref, wg_ref, bg_ref, lam_ref,
                  lng_ref, lnb_ref, ws_ref, bs_ref, wa_ref, wb_ref, wo_ref,
                  o_ref, xs_ref, a_ref, u_ref, hc_ref, *, tile, tiles_per_seq):
    seq_tile = pl.program_id(0) % tiles_per_seq

    @pl.when(seq_tile == 0)
    def _():
        xs_ref[0:SUBLANES, :] = jnp.zeros((SUBLANES, D_RNN), F32)
        hc_ref[...] = jnp.zeros((SUBLANES, D_RNN), F32)

    x = x_ref[...]
    h = _rmsnorm(x, ng_ref[...]).astype(BF16)

    def proj(lo, width):
        return _dot(h, win_ref[:, lo:lo + width])

    xs_ref[SUBLANES:SUBLANES + tile, :] = proj(OFF_XRNN, D_RNN)
    xr = cb_ref[...]
    for k in range(CONV_WIDTH):
        lo = SUBLANES - (CONV_WIDTH - 1) + k
        xr = xr + xs_ref[lo:lo + tile, :] * cw_ref[k:k + 1, :]
    xs_ref[0:SUBLANES, :] = xs_ref[tile:tile + SUBLANES, :]

    xr_bf = xr.astype(BF16)
    neg_lam = -lam_ref[...]
    softplus = jnp.maximum(neg_lam, 0.0) + jnp.log1p(jnp.exp(-jnp.abs(neg_lam)))
    for blk in range(N_GATE_BLOCKS):
        lo = blk * GATE_BLOCK
        z = _dot(xr_bf[:, lo:lo + GATE_BLOCK], wg_ref[blk]) + bg_ref[blk]
        r = jax.nn.sigmoid(z[:, :GATE_BLOCK])
        i = jax.nn.sigmoid(z[:, GATE_BLOCK:])
        log_a = (-LRU_C * r) * softplus[:, lo:lo + GATE_BLOCK]
        a = jnp.exp(log_a)
        a_ref[:, lo:lo + GATE_BLOCK] = a
        norm = jnp.sqrt(-jnp.tanh(log_a) * (1.0 + a * a))
        u_ref[:, lo:lo + GATE_BLOCK] = norm * (i * xr[:, lo:lo + GATE_BLOCK])

    row = lax.broadcasted_iota(jnp.int32, (SUBLANES, D_RNN), 0)

    def scan_group(g, carry):
        r0 = pl.multiple_of(g * SUBLANES, SUBLANES)
        a8 = a_ref[pl.ds(r0, SUBLANES), :]
        b8 = u_ref[pl.ds(r0, SUBLANES), :]
        for k in (1, 2, 4):
            keep = row >= k
            a_prev = jnp.where(keep, pltpu.roll(a8, k, 0), 1.0)
            b_prev = jnp.where(keep, pltpu.roll(b8, k, 0), 0.0)
            b8 = a8 * b_prev + b8
            a8 = a8 * a_prev
        h8 = a8 * carry + b8
        u_ref[pl.ds(r0, SUBLANES), :] = h8
        return jnp.broadcast_to(h8[SUBLANES - 1:SUBLANES, :], (SUBLANES, D_RNN))

    hc_ref[...] = lax.fori_loop(0, tile // SUBLANES, scan_group, hc_ref[...], unroll=2)

    ya = (u_ref[...] * jax.nn.gelu(proj(OFF_GRNN, D_RNN))).astype(BF16)
    ya = _dot(ya, wa_ref[...])

    gu = jax.nn.gelu(proj(OFF_U, D_SGU))
    gv = jax.nn.gelu(proj(OFF_V, D_SGU))
    mu = jnp.mean(gv, axis=-1, keepdims=True)
    vc = gv - mu
    var = jnp.mean(vc * vc, axis=-1, keepdims=True)
    vln = (vc * lax.rsqrt(var + EPS) * lng_ref[...] + lnb_ref[...]).astype(BF16)

    t_chunk = lax.broadcasted_iota(jnp.int32, (SGU_BLOCK, SGU_BLOCK), 0) // CHUNK
    s_chunk = lax.broadcasted_iota(jnp.int32, (SGU_BLOCK, SGU_BLOCK), 1) // CHUNK
    causal = t_chunk >= s_chunk
    ws = [jnp.where(causal, ws_ref[g], 0.0).astype(BF16) for g in range(SGU_GROUPS)]
    blocks = []
    for n in range(tile // SGU_BLOCK):
        rows = slice(n * SGU_BLOCK, (n + 1) * SGU_BLOCK)
        cols = [
            _dot(ws[g], vln[rows, g * SGU_GROUP_DIM:(g + 1) * SGU_GROUP_DIM])
            for g in range(SGU_GROUPS)
        ]
        blocks.append(jnp.concatenate(cols, axis=1) + bs_ref[...])
    mixed = jnp.concatenate(blocks, axis=0)
    yb = _dot((gu * mixed).astype(BF16), wb_ref[...])

    gate_a = jax.nn.sigmoid(proj(OFF_GATE_A, D_MODEL))
    gate_b = jax.nn.sigmoid(proj(OFF_GATE_B, D_MODEL))
    merged = (gate_a * ya + gate_b * yb).astype(BF16)
    o_ref[...] = x + _dot(merged, wo_ref[...])


def _ffn_kernel(x_ref, ng_ref, wu_ref, wd_ref, fg_ref, o_ref, *, final_norm):
    x = x_ref[...]
    h = _rmsnorm(x, ng_ref[...]).astype(BF16)
    acc = x
    for c in range(D_FF // FF_CHUNK):
        lo = c * FF_CHUNK
        f = jnp.maximum(_dot(h, wu_ref[:, lo:lo + FF_CHUNK]), 0.0)
        acc = acc + _dot((f * f).astype(BF16), wd_ref[lo:lo + FF_CHUNK, :])
    if final_norm:
        acc = _rmsnorm(acc, fg_ref[...])
    o_ref[...] = acc


def _resident(shape):
    ndim = len(shape)
    return pl.BlockSpec(shape, lambda i: (0,) * ndim, pipeline_mode=pl.Buffered(1))


def _mixer_call(x2, params, *, seq_len):
    n_tok = x2.shape[0]
    tile = MIXER_TILE
    kern = functools.partial(_mixer_kernel, tile=tile, tiles_per_seq=seq_len // tile)
    tok_spec = pl.BlockSpec((tile, D_MODEL), lambda i: (i, 0))
    return pl.pallas_call(
        kern,
        grid=(n_tok // tile,),
        in_specs=[tok_spec] + [_resident(p.shape) for p in params],
        out_specs=tok_spec,
        out_shape=jax.ShapeDtypeStruct((n_tok, D_MODEL), F32),
        scratch_shapes=[
            pltpu.VMEM((tile + SUBLANES, D_RNN), F32),
            pltpu.VMEM((tile, D_RNN), F32),
            pltpu.VMEM((tile, D_RNN), F32),
            pltpu.VMEM((SUBLANES, D_RNN), F32),
        ],
        compiler_params=pltpu.CompilerParams(
            dimension_semantics=("arbitrary",), vmem_limit_bytes=VMEM_LIMIT_BYTES),
        name="mixer",
    )(x2, *params)


def _ffn_call(x2, params, *, final_norm):
    n_tok = x2.shape[0]
    tile = FFN_TILE
    kern = functools.partial(_ffn_kernel, final_norm=final_norm)
    tok_spec = pl.BlockSpec((tile, D_MODEL), lambda i: (i, 0))
    return pl.pallas_call(
        kern,
        grid=(n_tok // tile,),
        in_specs=[tok_spec] + [_resident(p.shape) for p in params],
        out_specs=tok_spec,
        out_shape=jax.ShapeDtypeStruct((n_tok, D_MODEL), F32),
        compiler_params=pltpu.CompilerParams(
            dimension_semantics=("arbitrary",), vmem_limit_bytes=VMEM_LIMIT_BYTES),
        name="ffn",
    )(x2, *params)


def _gate_weights(w_a, w_x, b_a, b_x):
    eye = jnp.eye(HEADS_PER_GATE_BLOCK, dtype=w_a.dtype)

    def block_diag(w):
        w = w.reshape(N_GATE_BLOCKS, HEADS_PER_GATE_BLOCK, RNN_HEAD_DIM, RNN_HEAD_DIM)
        return jnp.einsum('jhik,hg->jhigk', w, eye).reshape(N_GATE_BLOCKS, GATE_BLOCK, GATE_BLOCK)

    wg = jnp.concatenate([block_diag(w_a), block_diag(w_x)], axis=-1).astype(BF16)
    bg = jnp.concatenate([b_a.reshape(N_GATE_BLOCKS, 1, GATE_BLOCK),
                          b_x.reshape(N_GATE_BLOCKS, 1, GATE_BLOCK)], axis=-1)
    return wg, bg


def kernel(x, norm_mix_g, w_in, conv_w, conv_b, lru_w_a, lru_b_a, lru_w_x, lru_b_x, lru_lambda, sgu_ln_g, sgu_ln_b, sgu_w_s, sgu_b_s, w_branch_a, w_branch_b, w_out, norm_ffn_g, w_up, w_down, final_norm_g):
    batch, seq_len, _ = x.shape
    depth = w_in.shape[0]
    assert seq_len % MIXER_TILE == 0 and (batch * seq_len) % FFN_TILE == 0
    x2 = x.reshape(batch * seq_len, D_MODEL)
    for l in range(depth):
        wg, bg = _gate_weights(lru_w_a[l], lru_w_x[l], lru_b_a[l], lru_b_x[l])
        bs = jnp.repeat(sgu_b_s[l].T, SGU_GROUP_DIM, axis=1)
        mixer_params = (
            norm_mix_g[l].reshape(1, D_MODEL),
            w_in[l].astype(BF16),
            conv_w[l],
            conv_b[l].reshape(1, D_RNN),
            wg, bg,
            lru_lambda[l].reshape(1, D_RNN),
            sgu_ln_g[l].reshape(1, D_SGU),
            sgu_ln_b[l].reshape(1, D_SGU),
            sgu_w_s[l],
            bs,
            w_branch_a[l].astype(BF16),
            w_branch_b[l].astype(BF16),
            w_out[l].astype(BF16),
        )
        x2 = _mixer_call(x2, mixer_params, seq_len=seq_len)
        ffn_params = (
            norm_ffn_g[l].reshape(1, D_MODEL),
            w_up[l].astype(BF16),
            w_down[l].astype(BF16),
            final_norm_g.reshape(1, D_MODEL),
        )
        x2 = _ffn_call(x2, ffn_params, final_norm=(l == depth - 1))
    return x2.reshape(batch, seq_len, D_MODEL)
```

```python
import functools

import jax
import jax.numpy as jnp
from jax import lax
from jax.experimental import pallas as pl
from jax.experimental.pallas import tpu as pltpu

D_MODEL = 1024
D_RNN = 1280
RNN_HEADS = 20
RNN_HEAD_DIM = D_RNN // RNN_HEADS
CONV_WIDTH = 4
LRU_C = 8.0
D_SGU = 1024
SGU_GROUPS = 8
SGU_GROUP_DIM = D_SGU // SGU_GROUPS
SGU_BLOCK = 128
CHUNK = 64
D_FF = 4 * D_MODEL
EPS = 1e-6

OFF_XRNN = 0
OFF_GRNN = OFF_XRNN + D_RNN
OFF_U = OFF_GRNN + D_RNN
OFF_V = OFF_U + D_SGU
OFF_GATE_A = OFF_V + D_SGU
OFF_GATE_B = OFF_GATE_A + D_MODEL
D_IN = OFF_GATE_B + D_MODEL

SUBLANES = 8
LANES = 128
N_LANE_TILES = D_RNN // LANES
GATE_BLOCK = 256
N_GATE_BLOCKS = D_RNN // GATE_BLOCK
HEADS_PER_GATE_BLOCK = GATE_BLOCK // RNN_HEAD_DIM
FF_CHUNK = 1024

MIXER_TILE = 256
FFN_TILE = 512
VMEM_LIMIT_BYTES = 56 * 1024 * 1024

F32 = jnp.float32
BF16 = jnp.bfloat16


def _dot(a, b):
    return jnp.dot(a, b, preferred_element_type=F32)


def _rmsnorm(x, g):
    return x * lax.rsqrt(jnp.mean(x * x, axis=-1, keepdims=True) + EPS) * g


def _gelu(x):
    c = 2.0 * 0.7978845608028654
    return x * jax.nn.sigmoid(x * (c + (c * 0.044715) * (x * x)))


def _sqrt_nonneg(y):
    return jnp.where(y > 0.0, y * lax.rsqrt(y), 0.0)


def _mixer_kernel(x_ref, ng_ref, win_ref, cw_ref, cb_ref, wg_ref, bg_ref, lam_ref,
                  lng_ref, lnb_ref, ws_ref, bs_ref, wa_ref, wb_ref, wo_ref,
                  o_ref, xs_ref, xr_ref, a_ref, u_ref, hc_ref, *, tile, tiles_per_seq):
    chain = tile // SUBLANES
    pitch = chain + SUBLANES
    head = SUBLANES
    seq_tile = pl.program_id(0) % tiles_per_seq

    def to_slab(ref, j, val, base=0):
        for s in range(SUBLANES):
            lo = base + s * pitch
            ref[j, lo:lo + chain, :] = val[s * chain:(s + 1) * chain, :]

    def from_slab(ref, j):
        return [ref[j, s * pitch:s * pitch + chain, :] for s in range(SUBLANES)]

    def load_step(ref, j, r, base=0):
        return ref[j, pl.ds(base + r, SUBLANES, stride=pitch), :]

    def store_step(ref, j, r, val):
        ref[j, pl.ds(r, SUBLANES, stride=pitch), :] = val

    @pl.when(seq_tile == 0)
    def _():
        xs_ref[:, 0:head, :] = jnp.zeros((N_LANE_TILES, head, LANES), F32)
        hc_ref[...] = jnp.zeros((N_LANE_TILES, SUBLANES, LANES), F32)

    x = x_ref[...]
    h = _rmsnorm(x, ng_ref[...]).astype(BF16)

    def proj(lo, width):
        return _dot(h, win_ref[:, lo:lo + width])

    x_rnn = proj(OFF_XRNN, D_RNN)
    for j in range(N_LANE_TILES):
        col = x_rnn[:, j * LANES:(j + 1) * LANES]
        to_slab(xs_ref, j, col, base=head)
        for s in range(SUBLANES):
            lo = head + s * pitch + chain
            xs_ref[j, lo:lo + SUBLANES, :] = col[(s + 1) * chain - SUBLANES:(s + 1) * chain, :]

    gu = _gelu(proj(OFF_U, D_SGU))
    gv = _gelu(proj(OFF_V, D_SGU))

    for j in range(N_LANE_TILES):
        lanes = slice(j * LANES, (j + 1) * LANES)
        bias = jnp.broadcast_to(cb_ref[:, lanes], (SUBLANES, LANES))
        taps = [jnp.broadcast_to(cw_ref[k:k + 1, lanes], (SUBLANES, LANES))
                for k in range(CONV_WIDTH)]
        window = [load_step(xs_ref, j, r, head) for r in range(-(CONV_WIDTH - 1), 0)]
        for r in range(chain):
            window.append(load_step(xs_ref, j, r, head))
            acc = bias
            for k in range(CONV_WIDTH):
                acc = acc + window[k] * taps[k]
            store_step(xr_ref, j, r, acc)
            window.pop(0)
        tail = head + SUBLANES * pitch - SUBLANES
        xs_ref[j, 0:head, :] = xs_ref[j, tail:tail + SUBLANES, :]

    neg_lam = -lam_ref[...]
    softplus = jnp.maximum(neg_lam, 0.0) + jnp.log1p(jnp.exp(-jnp.abs(neg_lam)))
    decay_rate = LRU_C * softplus
    tiles_per_block = GATE_BLOCK // LANES
    for blk in range(N_GATE_BLOCKS):
        lo = blk * GATE_BLOCK
        tiles = range(blk * tiles_per_block, (blk + 1) * tiles_per_block)
        xr = jnp.concatenate(
            [jnp.concatenate(from_slab(xr_ref, j), axis=0) for j in tiles], axis=1)
        z = _dot(xr.astype(BF16), wg_ref[blk]) + bg_ref[blk]
        r = jax.nn.sigmoid(z[:, :GATE_BLOCK])
        i = jax.nn.sigmoid(z[:, GATE_BLOCK:])
        neg_log_a = r * decay_rate[:, lo:lo + GATE_BLOCK]
        a = jnp.exp(-neg_log_a)
        norm = _sqrt_nonneg(jnp.tanh(neg_log_a) * (1.0 + a * a))
        u = norm * (i * xr)
        for t, j in enumerate(tiles):
            to_slab(a_ref, j, a[:, t * LANES:(t + 1) * LANES])
            to_slab(u_ref, j, u[:, t * LANES:(t + 1) * LANES])

    g_act = _gelu(proj(OFF_GRNN, D_RNN))

    row = lax.broadcasted_iota(jnp.int32, (SUBLANES, LANES), 0)
    ya_cols = []
    for j in range(N_LANE_TILES):
        hl = load_step(u_ref, j, 0)
        pr = load_step(a_ref, j, 0)
        for r in range(1, chain):
            a_r = load_step(a_ref, j, r)
            hl = a_r * hl + load_step(u_ref, j, r)
            pr = pr * a_r
            store_step(u_ref, j, r, hl)
            store_step(a_ref, j, r, pr)
        for k in (1, 2, 4):
            keep = row >= k
            p_prev = jnp.where(keep, pltpu.roll(pr, k, 0), 1.0)
            h_prev = jnp.where(keep, pltpu.roll(hl, k, 0), 0.0)
            hl = pr * h_prev + hl
            pr = pr * p_prev
        carry = hc_ref[j]
        ends = hl + pr * carry
        entry = jnp.where(row >= 1, pltpu.roll(ends, 1, 0), carry)
        hc_ref[j] = jnp.broadcast_to(ends[SUBLANES - 1:SUBLANES, :], (SUBLANES, LANES))
        local_states = from_slab(u_ref, j)
        decays = from_slab(a_ref, j)
        pieces = [
            local_states[s] + decays[s] * jnp.broadcast_to(entry[s:s + 1, :], (chain, LANES))
            for s in range(SUBLANES)
        ]
        ya_cols.append(jnp.concatenate(pieces, axis=0) * g_act[:, j * LANES:(j + 1) * LANES])
    ya = _dot(jnp.concatenate(ya_cols, axis=1).astype(BF16), wa_ref[...])

    mu = jnp.mean(gv, axis=-1, keepdims=True)
    vc = gv - mu
    var = jnp.mean(vc * vc, axis=-1, keepdims=True)
    vln = (vc * lax.rsqrt(var + EPS) * lng_ref[...] + lnb_ref[...]).astype(BF16)

    t_chunk = lax.broadcasted_iota(jnp.int32, (SGU_BLOCK, SGU_BLOCK), 0) // CHUNK
    s_chunk = lax.broadcasted_iota(jnp.int32, (SGU_BLOCK, SGU_BLOCK), 1) // CHUNK
    causal = t_chunk >= s_chunk
    ws = [jnp.where(causal, ws_ref[g], 0.0).astype(BF16) for g in range(SGU_GROUPS)]
    blocks = []
    for n in range(tile // SGU_BLOCK):
        rows = slice(n * SGU_BLOCK, (n + 1) * SGU_BLOCK)
        cols = [
            _dot(ws[g], vln[rows, g * SGU_GROUP_DIM:(g + 1) * SGU_GROUP_DIM])
            for g in range(SGU_GROUPS)
        ]
        blocks.append(jnp.concatenate(cols, axis=1) + bs_ref[...])
    mixed = jnp.concatenate(blocks, axis=0)
    yb = _dot((gu * mixed).astype(BF16), wb_ref[...])

    gate_a = jax.nn.sigmoid(proj(OFF_GATE_A, D_MODEL))
    gate_b = jax.nn.sigmoid(proj(OFF_GATE_B, D_MODEL))
    merged = (gate_a * ya + gate_b * yb).astype(BF16)
    o_ref[...] = x + _dot(merged, wo_ref[...])


def _ffn_kernel(x_ref, ng_ref, wu_ref, wd_ref, fg_ref, o_ref, *, final_norm):
    x = x_ref[...]
    h = _rmsnorm(x, ng_ref[...]).astype(BF16)
    acc = x
    for c in range(D_FF // FF_CHUNK):
        lo = c * FF_CHUNK
        f = jnp.maximum(_dot(h, wu_ref[:, lo:lo + FF_CHUNK]), 0.0)
        acc = acc + _dot((f * f).astype(BF16), wd_ref[lo:lo + FF_CHUNK, :])
    if final_norm:
        acc = _rmsnorm(acc, fg_ref[...])
    o_ref[...] = acc


def _resident(shape):
    ndim = len(shape)
    return pl.BlockSpec(shape, lambda i: (0,) * ndim, pipeline_mode=pl.Buffered(1))


def _mixer_call(x2, params, *, seq_len):
    n_tok = x2.shape[0]
    tile = MIXER_TILE
    slab_rows = SUBLANES * (tile // SUBLANES + SUBLANES)
    kern = functools.partial(_mixer_kernel, tile=tile, tiles_per_seq=seq_len // tile)
    tok_spec = pl.BlockSpec((tile, D_MODEL), lambda i: (i, 0))
    return pl.pallas_call(
        kern,
        grid=(n_tok // tile,),
        in_specs=[tok_spec] + [_resident(p.shape) for p in params],
        out_specs=tok_spec,
        out_shape=jax.ShapeDtypeStruct((n_tok, D_MODEL), F32),
        scratch_shapes=[
            pltpu.VMEM((N_LANE_TILES, SUBLANES + slab_rows, LANES), F32),
            pltpu.VMEM((N_LANE_TILES, slab_rows, LANES), F32),
            pltpu.VMEM((N_LANE_TILES, slab_rows, LANES), F32),
            pltpu.VMEM((N_LANE_TILES, slab_rows, LANES), F32),
            pltpu.VMEM((N_LANE_TILES, SUBLANES, LANES), F32),
        ],
        compiler_params=pltpu.CompilerParams(
            dimension_semantics=("arbitrary",), vmem_limit_bytes=VMEM_LIMIT_BYTES),
        name="mixer",
    )(x2, *params)


def _ffn_call(x2, params, *, final_norm):
    n_tok = x2.shape[0]
    tile = FFN_TILE
    kern = functools.partial(_ffn_kernel, final_norm=final_norm)
    tok_spec = pl.BlockSpec((tile, D_MODEL), lambda i: (i, 0))
    return pl.pallas_call(
        kern,
        grid=(n_tok // tile,),
        in_specs=[tok_spec] + [_resident(p.shape) for p in params],
        out_specs=tok_spec,
        out_shape=jax.ShapeDtypeStruct((n_tok, D_MODEL), F32),
        compiler_params=pltpu.CompilerParams(
            dimension_semantics=("arbitrary",), vmem_limit_bytes=VMEM_LIMIT_BYTES),
        name="ffn",
    )(x2, *params)


def _gate_weights(w_a, w_x, b_a, b_x):
    eye = jnp.eye(HEADS_PER_GATE_BLOCK, dtype=w_a.dtype)

    def block_diag(w):
        w = w.reshape(N_GATE_BLOCKS, HEADS_PER_GATE_BLOCK, RNN_HEAD_DIM, RNN_HEAD_DIM)
        return jnp.einsum('jhik,hg->jhigk', w, eye).reshape(N_GATE_BLOCKS, GATE_BLOCK, GATE_BLOCK)

    wg = jnp.concatenate([block_diag(w_a), block_diag(w_x)], axis=-1).astype(BF16)
    bg = jnp.concatenate([b_a.reshape(N_GATE_BLOCKS, 1, GATE_BLOCK),
                          b_x.reshape(N_GATE_BLOCKS, 1, GATE_BLOCK)], axis=-1)
    return wg, bg


def kernel(x, norm_mix_g, w_in, conv_w, conv_b, lru_w_a, lru_b_a, lru_w_x, lru_b_x, lru_lambda, sgu_ln_g, sgu_ln_b, sgu_w_s, sgu_b_s, w_branch_a, w_branch_b, w_out, norm_ffn_g, w_up, w_down, final_norm_g):
    batch, seq_len, _ = x.shape
    depth = w_in.shape[0]
    assert seq_len % MIXER_TILE == 0 and (batch * seq_len) % FFN_TILE == 0
    x2 = x.reshape(batch * seq_len, D_MODEL)
    for l in range(depth):
        wg, bg = _gate_weights(lru_w_a[l], lru_w_x[l], lru_b_a[l], lru_b_x[l])
        bs = jnp.repeat(sgu_b_s[l].T, SGU_GROUP_DIM, axis=1)
        mixer_params = (
            norm_mix_g[l].reshape(1, D_MODEL),
            w_in[l].astype(BF16),
            conv_w[l],
            conv_b[l].reshape(1, D_RNN),
            wg, bg,
            lru_lambda[l].reshape(1, D_RNN),
            sgu_ln_g[l].reshape(1, D_SGU),
            sgu_ln_b[l].reshape(1, D_SGU),
            sgu_w_s[l],
            bs,
            w_branch_a[l].astype(BF16),
            w_branch_b[l].astype(BF16),
            w_out[l].astype(BF16),
        )
        x2 = _mixer_call(x2, mixer_params, seq_len=seq_len)
        ffn_params = (
            norm_ffn_g[l].reshape(1, D_MODEL),
            w_up[l].astype(BF16),
            w_down[l].astype(BF16),
            final_norm_g.reshape(1, D_MODEL),
        )
        x2 = _ffn_call(x2, ffn_params, final_norm=(l == depth - 1))
    return x2.reshape(batch, seq_len, D_MODEL)
```

```python
import functools

import jax
import jax.numpy as jnp
from jax import lax
from jax.experimental import pallas as pl
from jax.experimental.pallas import tpu as pltpu

D_MODEL = 1024
D_RNN = 1280
RNN_HEADS = 20
RNN_HEAD_DIM = D_RNN // RNN_HEADS
CONV_WIDTH = 4
LRU_C = 8.0
D_SGU = 1024
SGU_GROUPS = 8
SGU_GROUP_DIM = D_SGU // SGU_GROUPS
SGU_BLOCK = 128
CHUNK = 64
D_FF = 4 * D_MODEL
EPS = 1e-6

OFF_XRNN = 0
OFF_GRNN = OFF_XRNN + D_RNN
OFF_U = OFF_GRNN + D_RNN
OFF_V = OFF_U + D_SGU
OFF_GATE_A = OFF_V + D_SGU
OFF_GATE_B = OFF_GATE_A + D_MODEL
D_IN = OFF_GATE_B + D_MODEL

SUBLANES = 8
LANES = 128
N_LANE_TILES = D_RNN // LANES
GATE_BLOCK = 256
N_GATE_BLOCKS = D_RNN // GATE_BLOCK
HEADS_PER_GATE_BLOCK = GATE_BLOCK // RNN_HEAD_DIM
FF_CHUNK = 1024

MIXER_TILE = 256
FFN_TILE = 512
VMEM_LIMIT_BYTES = 56 * 1024 * 1024

F32 = jnp.float32
BF16 = jnp.bfloat16


def _dot(a, b):
    return jnp.dot(a, b, preferred_element_type=F32)


def _rmsnorm(x, g):
    return x * lax.rsqrt(jnp.mean(x * x, axis=-1, keepdims=True) + EPS) * g


def _gelu(x):
    c = 2.0 * 0.7978845608028654
    return x * jax.nn.sigmoid(x * (c + (c * 0.044715) * (x * x)))


def _sqrt_nonneg(y):
    return jnp.where(y > 0.0, y * lax.rsqrt(y), 0.0)


def _mixer_kernel(x_ref, ng_ref, win_ref, cw_ref, cb_ref, wg_ref, bg_ref, lam_ref,
                  lng_ref, lnb_ref, ws_ref, bs_ref, wa_ref, wb_ref, wo_ref,
                  o_ref, h_ref, xs_ref, xr_ref, a_ref, u_ref, hc_ref, *, tile, tiles_per_seq):
    chain = tile // SUBLANES
    pitch = chain + SUBLANES
    head = SUBLANES
    seq_tile = pl.program_id(0) % tiles_per_seq

    def to_slab(ref, j, val, base=0):
        for s in range(SUBLANES):
            lo = base + s * pitch
            ref[j, lo:lo + chain, :] = val[s * chain:(s + 1) * chain, :]

    def from_slab(ref, j):
        return [ref[j, s * pitch:s * pitch + chain, :] for s in range(SUBLANES)]

    def load_step(ref, j, r, base=0):
        return ref[j, pl.ds(base + r, SUBLANES, stride=pitch), :]

    def store_step(ref, j, r, val):
        ref[j, pl.ds(r, SUBLANES, stride=pitch), :] = val

    @pl.when(seq_tile == 0)
    def _():
        xs_ref[:, 0:head, :] = jnp.zeros((N_LANE_TILES, head, LANES), F32)
        hc_ref[...] = jnp.zeros((N_LANE_TILES, SUBLANES, LANES), F32)

    x = x_ref[...]
    h_ref[...] = _rmsnorm(x, ng_ref[...]).astype(BF16)

    def proj(lo, width):
        return _dot(h_ref[...], win_ref[:, lo:lo + width])

    x_rnn = proj(OFF_XRNN, D_RNN)
    for j in range(N_LANE_TILES):
        col = x_rnn[:, j * LANES:(j + 1) * LANES]
        to_slab(xs_ref, j, col, base=head)
        for s in range(SUBLANES):
            lo = head + s * pitch + chain
            xs_ref[j, lo:lo + SUBLANES, :] = col[(s + 1) * chain - SUBLANES:(s + 1) * chain, :]

    v_raw = proj(OFF_V, D_SGU)

    for j in range(N_LANE_TILES):
        lanes = slice(j * LANES, (j + 1) * LANES)
        bias = jnp.broadcast_to(cb_ref[:, lanes], (SUBLANES, LANES))
        taps = [jnp.broadcast_to(cw_ref[k:k + 1, lanes], (SUBLANES, LANES))
                for k in range(CONV_WIDTH)]
        window = [load_step(xs_ref, j, r, head) for r in range(-(CONV_WIDTH - 1), 0)]
        for r in range(chain):
            window.append(load_step(xs_ref, j, r, head))
            acc = bias
            for k in range(CONV_WIDTH):
                acc = acc + window[k] * taps[k]
            store_step(xr_ref, j, r, acc)
            window.pop(0)
        tail = head + SUBLANES * pitch - SUBLANES
        xs_ref[j, 0:head, :] = xs_ref[j, tail:tail + SUBLANES, :]

    gv = _gelu(v_raw)
    mu = jnp.mean(gv, axis=-1, keepdims=True)
    vc = gv - mu
    var = jnp.mean(vc * vc, axis=-1, keepdims=True)
    vln = (vc * lax.rsqrt(var + EPS) * lng_ref[...] + lnb_ref[...]).astype(BF16)
    u_raw = proj(OFF_U, D_SGU)

    piece = GATE_BLOCK
    piece_offsets = ([OFF_GRNN + c for c in range(0, D_RNN, piece)]
                     + [OFF_GATE_A + c for c in range(0, 2 * D_MODEL, piece)])
    per_block = -(-len(piece_offsets) // N_GATE_BLOCKS)
    pieces_out = []

    neg_lam = -lam_ref[...]
    softplus = jnp.maximum(neg_lam, 0.0) + jnp.log1p(jnp.exp(-jnp.abs(neg_lam)))
    decay_rate = LRU_C * softplus
    tiles_per_block = GATE_BLOCK // LANES
    for blk in range(N_GATE_BLOCKS):
        lo = blk * GATE_BLOCK
        tiles = range(blk * tiles_per_block, (blk + 1) * tiles_per_block)
        xr = jnp.concatenate(
            [jnp.concatenate(from_slab(xr_ref, j), axis=0) for j in tiles], axis=1)
        z = _dot(xr.astype(BF16), wg_ref[blk]) + bg_ref[blk]
        for off in piece_offsets[blk * per_block:(blk + 1) * per_block]:
            pieces_out.append(proj(off, piece))
        r = jax.nn.sigmoid(z[:, :GATE_BLOCK])
        i = jax.nn.sigmoid(z[:, GATE_BLOCK:])
        neg_log_a = r * decay_rate[:, lo:lo + GATE_BLOCK]
        a = jnp.exp(-neg_log_a)
        norm = _sqrt_nonneg(jnp.tanh(neg_log_a) * (1.0 + a * a))
        u = norm * (i * xr)
        for t, j in enumerate(tiles):
            to_slab(a_ref, j, a[:, t * LANES:(t + 1) * LANES])
            to_slab(u_ref, j, u[:, t * LANES:(t + 1) * LANES])
    n_g = D_RNN // piece
    n_gate = D_MODEL // piece
    g_act = _gelu(jnp.concatenate(pieces_out[:n_g], axis=1))
    gate_a_raw = jnp.concatenate(pieces_out[n_g:n_g + n_gate], axis=1)
    gate_b_raw = jnp.concatenate(pieces_out[n_g + n_gate:], axis=1)

    t_chunk = lax.broadcasted_iota(jnp.int32, (SGU_BLOCK, SGU_BLOCK), 0) // CHUNK
    s_chunk = lax.broadcasted_iota(jnp.int32, (SGU_BLOCK, SGU_BLOCK), 1) // CHUNK
    causal = t_chunk >= s_chunk
    ws = [jnp.where(causal, ws_ref[g], 0.0).astype(BF16) for g in range(SGU_GROUPS)]
    blocks = []
    for n in range(tile // SGU_BLOCK):
        rows = slice(n * SGU_BLOCK, (n + 1) * SGU_BLOCK)
        cols = [
            _dot(ws[g], vln[rows, g * SGU_GROUP_DIM:(g + 1) * SGU_GROUP_DIM])
            for g in range(SGU_GROUPS)
        ]
        blocks.append(jnp.concatenate(cols, axis=1) + bs_ref[...])
    mixed = jnp.concatenate(blocks, axis=0)

    row = lax.broadcasted_iota(jnp.int32, (SUBLANES, LANES), 0)
    ya_cols = []
    for j in range(N_LANE_TILES):
        hl = load_step(u_ref, j, 0)
        pr = load_step(a_ref, j, 0)
        for r in range(1, chain):
            a_r = load_step(a_ref, j, r)
            hl = a_r * hl + load_step(u_ref, j, r)
            pr = pr * a_r
            store_step(u_ref, j, r, hl)
            store_step(a_ref, j, r, pr)
        for k in (1, 2, 4):
            keep = row >= k
            p_prev = jnp.where(keep, pltpu.roll(pr, k, 0), 1.0)
            h_prev = jnp.where(keep, pltpu.roll(hl, k, 0), 0.0)
            hl = pr * h_prev + hl
            pr = pr * p_prev
        carry = hc_ref[j]
        ends = hl + pr * carry
        entry = jnp.where(row >= 1, pltpu.roll(ends, 1, 0), carry)
        hc_ref[j] = jnp.broadcast_to(ends[SUBLANES - 1:SUBLANES, :], (SUBLANES, LANES))
        local_states = from_slab(u_ref, j)
        decays = from_slab(a_ref, j)
        states = [
            local_states[s] + decays[s] * jnp.broadcast_to(entry[s:s + 1, :], (chain, LANES))
            for s in range(SUBLANES)
        ]
        ya_cols.append(jnp.concatenate(states, axis=0) * g_act[:, j * LANES:(j + 1) * LANES])
    ya = _dot(jnp.concatenate(ya_cols, axis=1).astype(BF16), wa_ref[...])

    yb = _dot((_gelu(u_raw) * mixed).astype(BF16), wb_ref[...])

    merged = (jax.nn.sigmoid(gate_a_raw) * ya + jax.nn.sigmoid(gate_b_raw) * yb).astype(BF16)
    o_ref[...] = x + _dot(merged, wo_ref[...])


def _ffn_kernel(x_ref, ng_ref, wu_ref, wd_ref, fg_ref, o_ref, h_ref, *, final_norm):
    x = x_ref[...]
    h_ref[...] = _rmsnorm(x, ng_ref[...]).astype(BF16)
    acc = x
    for c in range(D_FF // FF_CHUNK):
        lo = c * FF_CHUNK
        f = jnp.maximum(_dot(h_ref[...], wu_ref[:, lo:lo + FF_CHUNK]), 0.0)
        acc = acc + _dot((f * f).astype(BF16), wd_ref[lo:lo + FF_CHUNK, :])
    if final_norm:
        acc = _rmsnorm(acc, fg_ref[...])
    o_ref[...] = acc


def _resident(shape):
    ndim = len(shape)
    return pl.BlockSpec(shape, lambda i: (0,) * ndim, pipeline_mode=pl.Buffered(1))


def _mixer_call(x2, params, *, seq_len):
    n_tok = x2.shape[0]
    tile = MIXER_TILE
    slab_rows = SUBLANES * (tile // SUBLANES + SUBLANES)
    kern = functools.partial(_mixer_kernel, tile=tile, tiles_per_seq=seq_len // tile)
    tok_spec = pl.BlockSpec((tile, D_MODEL), lambda i: (i, 0))
    return pl.pallas_call(
        kern,
        grid=(n_tok // tile,),
        in_specs=[tok_spec] + [_resident(p.shape) for p in params],
        out_specs=tok_spec,
        out_shape=jax.ShapeDtypeStruct((n_tok, D_MODEL), F32),
        scratch_shapes=[
            pltpu.VMEM((tile, D_MODEL), BF16),
            pltpu.VMEM((N_LANE_TILES, SUBLANES + slab_rows, LANES), F32),
            pltpu.VMEM((N_LANE_TILES, slab_rows, LANES), F32),
            pltpu.VMEM((N_LANE_TILES, slab_rows, LANES), F32),
            pltpu.VMEM((N_LANE_TILES, slab_rows, LANES), F32),
            pltpu.VMEM((N_LANE_TILES, SUBLANES, LANES), F32),
        ],
        compiler_params=pltpu.CompilerParams(
            dimension_semantics=("arbitrary",), vmem_limit_bytes=VMEM_LIMIT_BYTES),
        name="mixer",
    )(x2, *params)


def _ffn_call(x2, params, *, final_norm):
    n_tok = x2.shape[0]
    tile = FFN_TILE
    kern = functools.partial(_ffn_kernel, final_norm=final_norm)
    tok_spec = pl.BlockSpec((tile, D_MODEL), lambda i: (i, 0))
    return pl.pallas_call(
        kern,
        grid=(n_tok // tile,),
        in_specs=[tok_spec] + [_resident(p.shape) for p in params],
        out_specs=tok_spec,
        out_shape=jax.ShapeDtypeStruct((n_tok, D_MODEL), F32),
        scratch_shapes=[pltpu.VMEM((tile, D_MODEL), BF16)],
        compiler_params=pltpu.CompilerParams(
            dimension_semantics=("arbitrary",), vmem_limit_bytes=VMEM_LIMIT_BYTES),
        name="ffn",
    )(x2, *params)


def _gate_weights(w_a, w_x, b_a, b_x):
    eye = jnp.eye(HEADS_PER_GATE_BLOCK, dtype=w_a.dtype)

    def block_diag(w):
        w = w.reshape(N_GATE_BLOCKS, HEADS_PER_GATE_BLOCK, RNN_HEAD_DIM, RNN_HEAD_DIM)
        return jnp.einsum('jhik,hg->jhigk', w, eye).reshape(N_GATE_BLOCKS, GATE_BLOCK, GATE_BLOCK)

    wg = jnp.concatenate([block_diag(w_a), block_diag(w_x)], axis=-1).astype(BF16)
    bg = jnp.concatenate([b_a.reshape(N_GATE_BLOCKS, 1, GATE_BLOCK),
                          b_x.reshape(N_GATE_BLOCKS, 1, GATE_BLOCK)], axis=-1)
    return wg, bg


def kernel(x, norm_mix_g, w_in, conv_w, conv_b, lru_w_a, lru_b_a, lru_w_x, lru_b_x, lru_lambda, sgu_ln_g, sgu_ln_b, sgu_w_s, sgu_b_s, w_branch_a, w_branch_b, w_out, norm_ffn_g, w_up, w_down, final_norm_g):
    batch, seq_len, _ = x.shape
    depth = w_in.shape[0]
    assert seq_len % MIXER_TILE == 0 and (batch * seq_len) % FFN_TILE == 0
    x2 = x.reshape(batch * seq_len, D_MODEL)
    for l in range(depth):
        wg, bg = _gate_weights(lru_w_a[l], lru_w_x[l], lru_b_a[l], lru_b_x[l])
        bs = jnp.repeat(sgu_b_s[l].T, SGU_GROUP_DIM, axis=1)
        mixer_params = (
            norm_mix_g[l].reshape(1, D_MODEL),
            w_in[l].astype(BF16),
            conv_w[l],
            conv_b[l].reshape(1, D_RNN),
            wg, bg,
            lru_lambda[l].reshape(1, D_RNN),
            sgu_ln_g[l].reshape(1, D_SGU),
            sgu_ln_b[l].reshape(1, D_SGU),
            sgu_w_s[l],
            bs,
            w_branch_a[l].astype(BF16),
            w_branch_b[l].astype(BF16),
            w_out[l].astype(BF16),
        )
        x2 = _mixer_call(x2, mixer_params, seq_len=seq_len)
        ffn_params = (
            norm_ffn_g[l].reshape(1, D_MODEL),
            w_up[l].astype(BF16),
            w_down[l].astype(BF16),
            final_norm_g.reshape(1, D_MODEL),
        )
        x2 = _ffn_call(x2, ffn_params, final_norm=(l == depth - 1))
    return x2.reshape(batch, seq_len, D_MODEL)
```

```python
import functools

import numpy as np
import jax
import jax.numpy as jnp
from jax import lax
from jax.experimental import pallas as pl
from jax.experimental.pallas import tpu as pltpu

D_MODEL = 1024
D_RNN = 1280
RNN_HEADS = 20
RNN_HEAD_DIM = D_RNN // RNN_HEADS
CONV_WIDTH = 4
LRU_C = 8.0
D_SGU = 1024
SGU_GROUPS = 8
SGU_GROUP_DIM = D_SGU // SGU_GROUPS
SGU_BLOCK = 128
CHUNK = 64
D_FF = 4 * D_MODEL
EPS = 1e-6

OFF_XRNN = 0
OFF_GRNN = OFF_XRNN + D_RNN
OFF_U = OFF_GRNN + D_RNN
OFF_V = OFF_U + D_SGU
OFF_GATE_A = OFF_V + D_SGU
OFF_GATE_B = OFF_GATE_A + D_MODEL
D_IN = OFF_GATE_B + D_MODEL

SUBLANES = 8
GATE_BLOCK = 256
N_GATE_BLOCKS = D_RNN // GATE_BLOCK
HEADS_PER_GATE_BLOCK = GATE_BLOCK // RNN_HEAD_DIM
CONV_TAIL = (CONV_WIDTH - 1) * SUBLANES
FF_CHUNK = 1024

MIXER_TILE = 256
FFN_TILE = 512
VMEM_LIMIT_BYTES = 56 * 1024 * 1024

F32 = jnp.float32
BF16 = jnp.bfloat16


def _dot(a, b):
    return jnp.dot(a, b, preferred_element_type=F32)


def _rmsnorm(x, g):
    return x * lax.rsqrt(jnp.mean(x * x, axis=-1, keepdims=True) + EPS) * g


def _gelu(x):
    c = 2.0 * 0.7978845608028654
    return x * jax.nn.sigmoid(x * (c + (c * 0.044715) * (x * x)))


def _sqrt_nonneg(y):
    return jnp.where(y > 0.0, y * lax.rsqrt(y), 0.0)


def _shift_chains(v):
    return pltpu.roll(v, 1, 0)


def _mixer_kernel(x_ref, perm_ref, unperm_ref, ng_ref, win_ref, cw_ref, cb_ref, wg_ref, bg_ref,
                  lam_ref, lng_ref, lnb_ref, ws_ref, bs_ref, wa_ref, wb_ref, wo_ref,
                  o_ref, h_ref, hp_ref, tail_ref, hc_ref, *, tile, tiles_per_seq):
    steps = tile // SUBLANES
    seq_tile = pl.program_id(0) % tiles_per_seq

    @pl.when(seq_tile == 0)
    def _():
        tail_ref[...] = jnp.zeros((CONV_TAIL, D_RNN), F32)
        hc_ref[...] = jnp.zeros((SUBLANES, D_RNN), F32)

    x = x_ref[...]
    h_ref[...] = _rmsnorm(x, ng_ref[...]).astype(BF16)
    hp_ref[...] = _dot(perm_ref[...], h_ref[...]).astype(BF16)

    def proj(lo, width):
        return _dot(h_ref[...], win_ref[:, lo:lo + width])

    def proj_tm(lo, width):
        return _dot(hp_ref[...], win_ref[:, lo:lo + width])

    x_rnn = proj_tm(OFF_XRNN, D_RNN)
    v_raw = proj(OFF_V, D_SGU)

    tail = x_rnn[tile - CONV_TAIL:, :]
    row = lax.broadcasted_iota(jnp.int32, (SUBLANES, D_RNN), 0)
    prev_tail = tail_ref[...]
    head = [
        _shift_chains(jnp.where(row == SUBLANES - 1,
                                prev_tail[q * SUBLANES:(q + 1) * SUBLANES, :],
                                tail[q * SUBLANES:(q + 1) * SUBLANES, :]))
        for q in range(CONV_WIDTH - 1)
    ]
    tail_ref[...] = tail
    x_ext = jnp.concatenate(head + [x_rnn], axis=0)
    xr = cb_ref[...]
    for k in range(CONV_WIDTH):
        xr = xr + x_ext[k * SUBLANES:k * SUBLANES + tile, :] * cw_ref[k:k + 1, :]

    gv = _gelu(v_raw)
    mu = jnp.mean(gv, axis=-1, keepdims=True)
    vc = gv - mu
    var = jnp.mean(vc * vc, axis=-1, keepdims=True)
    vln = (vc * lax.rsqrt(var + EPS) * lng_ref[...] + lnb_ref[...]).astype(BF16)
    u_raw = proj(OFF_U, D_SGU)

    piece = GATE_BLOCK
    gate_offsets = [OFF_GATE_A + c for c in range(0, 2 * D_MODEL, piece)]
    per_block = -(-len(gate_offsets) // N_GATE_BLOCKS)
    gate_pieces = []

    neg_lam = -lam_ref[...]
    softplus = jnp.maximum(neg_lam, 0.0) + jnp.log1p(jnp.exp(-jnp.abs(neg_lam)))
    decay_rate = LRU_C * softplus
    row_b = lax.broadcasted_iota(jnp.int32, (SUBLANES, GATE_BLOCK), 0)
    ya_blocks = []
    for blk in range(N_GATE_BLOCKS):
        lo = blk * GATE_BLOCK
        xr_b = xr[:, lo:lo + GATE_BLOCK]
        z = _dot(xr_b.astype(BF16), wg_ref[blk]) + bg_ref[blk]
        g_rnn = proj_tm(OFF_GRNN + lo, GATE_BLOCK)
        for off in gate_offsets[blk * per_block:(blk + 1) * per_block]:
            gate_pieces.append(proj(off, piece))
        r = jax.nn.sigmoid(z[:, :GATE_BLOCK])
        i = jax.nn.sigmoid(z[:, GATE_BLOCK:])
        neg_log_a = r * decay_rate[:, lo:lo + GATE_BLOCK]
        a = jnp.exp(-neg_log_a)
        norm = _sqrt_nonneg(jnp.tanh(neg_log_a) * (1.0 + a * a))
        u = norm * (i * xr_b)

        hl = u[0:SUBLANES, :]
        pr = a[0:SUBLANES, :]
        local_states = [hl]
        decays = [pr]
        for s in range(1, steps):
            a_s = a[s * SUBLANES:(s + 1) * SUBLANES, :]
            hl = a_s * hl + u[s * SUBLANES:(s + 1) * SUBLANES, :]
            pr = pr * a_s
            local_states.append(hl)
            decays.append(pr)
        for k in (1, 2, 4):
            keep = row_b >= k
            p_prev = jnp.where(keep, pltpu.roll(pr, k, 0), 1.0)
            h_prev = jnp.where(keep, pltpu.roll(hl, k, 0), 0.0)
            hl = pr * h_prev + hl
            pr = pr * p_prev
        carry = hc_ref[:, lo:lo + GATE_BLOCK]
        ends = hl + pr * carry
        entry = jnp.where(row_b >= 1, _shift_chains(ends), carry)
        hc_ref[:, lo:lo + GATE_BLOCK] = jnp.broadcast_to(
            ends[SUBLANES - 1:SUBLANES, :], (SUBLANES, GATE_BLOCK))
        states = (jnp.concatenate(local_states, axis=0)
                  + jnp.concatenate(decays, axis=0) * jnp.concatenate([entry] * steps, axis=0))
        ya_blocks.append((states * _gelu(g_rnn)).astype(BF16))
    gate_a_raw = jnp.concatenate(gate_pieces[:D_MODEL // piece], axis=1)
    gate_b_raw = jnp.concatenate(gate_pieces[D_MODEL // piece:], axis=1)

    t_chunk = lax.broadcasted_iota(jnp.int32, (SGU_BLOCK, SGU_BLOCK), 0) // CHUNK
    s_chunk = lax.broadcasted_iota(jnp.int32, (SGU_BLOCK, SGU_BLOCK), 1) // CHUNK
    causal = t_chunk >= s_chunk
    ws = [jnp.where(causal, ws_ref[g], 0.0).astype(BF16) for g in range(SGU_GROUPS)]
    blocks = []
    for n in range(tile // SGU_BLOCK):
        rows = slice(n * SGU_BLOCK, (n + 1) * SGU_BLOCK)
        cols = [
            _dot(ws[g], vln[rows, g * SGU_GROUP_DIM:(g + 1) * SGU_GROUP_DIM])
            for g in range(SGU_GROUPS)
        ]
        blocks.append(jnp.concatenate(cols, axis=1) + bs_ref[...])
    mixed = jnp.concatenate(blocks, axis=0)

    ya_tok = _dot(unperm_ref[...], jnp.concatenate(ya_blocks, axis=1)).astype(BF16)
    ya = _dot(ya_tok, wa_ref[...])
    yb = _dot((_gelu(u_raw) * mixed).astype(BF16), wb_ref[...])

    merged = (jax.nn.sigmoid(gate_a_raw) * ya + jax.nn.sigmoid(gate_b_raw) * yb).astype(BF16)
    o_ref[...] = x + _dot(merged, wo_ref[...])


def _ffn_kernel(x_ref, ng_ref, wu_ref, wd_ref, fg_ref, o_ref, h_ref, *, final_norm):
    x = x_ref[...]
    h_ref[...] = _rmsnorm(x, ng_ref[...]).astype(BF16)
    acc = x
    for c in range(D_FF // FF_CHUNK):
        lo = c * FF_CHUNK
        f = jnp.maximum(_dot(h_ref[...], wu_ref[:, lo:lo + FF_CHUNK]), 0.0)
        acc = acc + _dot((f * f).astype(BF16), wd_ref[lo:lo + FF_CHUNK, :])
    if final_norm:
        acc = _rmsnorm(acc, fg_ref[...])
    o_ref[...] = acc


def _resident(shape):
    ndim = len(shape)
    return pl.BlockSpec(shape, lambda i: (0,) * ndim, pipeline_mode=pl.Buffered(1))


def _time_major_permutation(tile):
    steps = tile // SUBLANES
    dst = np.arange(tile)
    src = (dst % SUBLANES) * steps + dst // SUBLANES
    perm = np.zeros((tile, tile), np.float32)
    perm[dst, src] = 1.0
    return perm


def _mixer_call(x2, params, *, seq_len):
    n_tok = x2.shape[0]
    tile = MIXER_TILE
    perm = _time_major_permutation(tile)
    params = (jnp.asarray(perm, BF16), jnp.asarray(perm.T, BF16)) + tuple(params)
    kern = functools.partial(_mixer_kernel, tile=tile, tiles_per_seq=seq_len // tile)
    tok_spec = pl.BlockSpec((tile, D_MODEL), lambda i: (i, 0))
    return pl.pallas_call(
        kern,
        grid=(n_tok // tile,),
        in_specs=[tok_spec] + [_resident(p.shape) for p in params],
        out_specs=tok_spec,
        out_shape=jax.ShapeDtypeStruct((n_tok, D_MODEL), F32),
        scratch_shapes=[
            pltpu.VMEM((tile, D_MODEL), BF16),
            pltpu.VMEM((tile, D_MODEL), BF16),
            pltpu.VMEM((CONV_TAIL, D_RNN), F32),
            pltpu.VMEM((SUBLANES, D_RNN), F32),
        ],
        compiler_params=pltpu.CompilerParams(
            dimension_semantics=("arbitrary",), vmem_limit_bytes=VMEM_LIMIT_BYTES),
        name="mixer",
    )(x2, *params)


def _ffn_call(x2, params, *, final_norm):
    n_tok = x2.shape[0]
    tile = FFN_TILE
    kern = functools.partial(_ffn_kernel, final_norm=final_norm)
    tok_spec = pl.BlockSpec((tile, D_MODEL), lambda i: (i, 0))
    return pl.pallas_call(
        kern,
        grid=(n_tok // tile,),
        in_specs=[tok_spec] + [_resident(p.shape) for p in params],
        out_specs=tok_spec,
        out_shape=jax.ShapeDtypeStruct((n_tok, D_MODEL), F32),
        scratch_shapes=[pltpu.VMEM((tile, D_MODEL), BF16)],
        compiler_params=pltpu.CompilerParams(
            dimension_semantics=("arbitrary",), vmem_limit_bytes=VMEM_LIMIT_BYTES),
        name="ffn",
    )(x2, *params)


def _gate_weights(w_a, w_x, b_a, b_x):
    eye = jnp.eye(HEADS_PER_GATE_BLOCK, dtype=w_a.dtype)

    def block_diag(w):
        w = w.reshape(N_GATE_BLOCKS, HEADS_PER_GATE_BLOCK, RNN_HEAD_DIM, RNN_HEAD_DIM)
        return jnp.einsum('jhik,hg->jhigk', w, eye).reshape(N_GATE_BLOCKS, GATE_BLOCK, GATE_BLOCK)

    wg = jnp.concatenate([block_diag(w_a), block_diag(w_x)], axis=-1).astype(BF16)
    bg = jnp.concatenate([b_a.reshape(N_GATE_BLOCKS, 1, GATE_BLOCK),
                          b_x.reshape(N_GATE_BLOCKS, 1, GATE_BLOCK)], axis=-1)
    return wg, bg


def kernel(x, norm_mix_g, w_in, conv_w, conv_b, lru_w_a, lru_b_a, lru_w_x, lru_b_x, lru_lambda, sgu_ln_g, sgu_ln_b, sgu_w_s, sgu_b_s, w_branch_a, w_branch_b, w_out, norm_ffn_g, w_up, w_down, final_norm_g):
    batch, seq_len, _ = x.shape
    depth = w_in.shape[0]
    assert seq_len % MIXER_TILE == 0 and (batch * seq_len) % FFN_TILE == 0
    x2 = x.reshape(batch * seq_len, D_MODEL)
    for l in range(depth):
        wg, bg = _gate_weights(lru_w_a[l], lru_w_x[l], lru_b_a[l], lru_b_x[l])
        bs = jnp.repeat(sgu_b_s[l].T, SGU_GROUP_DIM, axis=1)
        mixer_params = (
            norm_mix_g[l].reshape(1, D_MODEL),
            w_in[l].astype(BF16),
            conv_w[l],
            conv_b[l].reshape(1, D_RNN),
            wg, bg,
            lru_lambda[l].reshape(1, D_RNN),
            sgu_ln_g[l].reshape(1, D_SGU),
            sgu_ln_b[l].reshape(1, D_SGU),
            sgu_w_s[l],
            bs,
            w_branch_a[l].astype(BF16),
            w_branch_b[l].astype(BF16),
            w_out[l].astype(BF16),
        )
        x2 = _mixer_call(x2, mixer_params, seq_len=seq_len)
        ffn_params = (
            norm_ffn_g[l].reshape(1, D_MODEL),
            w_up[l].astype(BF16),
            w_down[l].astype(BF16),
            final_norm_g.reshape(1, D_MODEL),
        )
        x2 = _ffn_call(x2, ffn_params, final_norm=(l == depth - 1))
    return x2.reshape(batch, seq_len, D_MODEL)
```

```python
import functools

import numpy as np
import jax
import jax.numpy as jnp
from jax import lax
from jax.experimental import pallas as pl
from jax.experimental.pallas import tpu as pltpu

D_MODEL = 1024
D_RNN = 1280
RNN_HEADS = 20
RNN_HEAD_DIM = D_RNN // RNN_HEADS
CONV_WIDTH = 4
LRU_C = 8.0
D_SGU = 1024
SGU_GROUPS = 8
SGU_GROUP_DIM = D_SGU // SGU_GROUPS
SGU_BLOCK = 128
CHUNK = 64
D_FF = 4 * D_MODEL
EPS = 1e-6

OFF_XRNN = 0
OFF_GRNN = OFF_XRNN + D_RNN
OFF_U = OFF_GRNN + D_RNN
OFF_V = OFF_U + D_SGU
OFF_GATE_A = OFF_V + D_SGU
OFF_GATE_B = OFF_GATE_A + D_MODEL
D_IN = OFF_GATE_B + D_MODEL

SUBLANES = 8
GATE_BLOCK = 256
N_GATE_BLOCKS = D_RNN // GATE_BLOCK
HEADS_PER_GATE_BLOCK = GATE_BLOCK // RNN_HEAD_DIM
CONV_TAIL = (CONV_WIDTH - 1) * SUBLANES
FF_CHUNK = 1024

MIXER_TILE = 256
FFN_TILE = 512
VMEM_LIMIT_BYTES = 56 * 1024 * 1024

F32 = jnp.float32
BF16 = jnp.bfloat16


def _dot(a, b):
    return jnp.dot(a, b, preferred_element_type=F32)


def _rmsnorm(x, g):
    return x * lax.rsqrt(jnp.mean(x * x, axis=-1, keepdims=True) + EPS) * g


def _gelu(x):
    c = 2.0 * 0.7978845608028654
    return x * jax.nn.sigmoid(x * (c + (c * 0.044715) * (x * x)))


def _sqrt_nonneg(y):
    return jnp.where(y > 0.0, y * lax.rsqrt(y), 0.0)


def _shift_chains(v):
    return pltpu.roll(v, 1, 0)


def _mixer_kernel(x_ref, perm_ref, unperm_ref, ng_ref, win_ref, cw_ref, cb_ref, wg_ref, bg_ref,
                  lam_ref, lng_ref, lnb_ref, ws_ref, bs_ref, wa_ref, wb_ref, wo_ref,
                  o_ref, h_ref, hp_ref, tail_ref, hc_ref, *, tile, tiles_per_seq):
    steps = tile // SUBLANES
    seq_tile = pl.program_id(0) % tiles_per_seq

    @pl.when(seq_tile == 0)
    def _():
        tail_ref[...] = jnp.zeros((CONV_TAIL, D_RNN), F32)
        hc_ref[...] = jnp.zeros((SUBLANES, D_RNN), F32)

    x = x_ref[...]
    h_ref[...] = _rmsnorm(x, ng_ref[...]).astype(BF16)
    hp_ref[...] = _dot(perm_ref[...], h_ref[...]).astype(BF16)

    def proj(lo, width):
        return _dot(h_ref[...], win_ref[:, lo:lo + width])

    def proj_tm(lo, width):
        return _dot(hp_ref[...], win_ref[:, lo:lo + width])

    x_rnn = proj_tm(OFF_XRNN, D_RNN)
    v_raw = proj(OFF_V, D_SGU)

    tail = x_rnn[tile - CONV_TAIL:, :]
    row = lax.broadcasted_iota(jnp.int32, (SUBLANES, D_RNN), 0)
    prev_tail = tail_ref[...]
    head = [
        _shift_chains(jnp.where(row == SUBLANES - 1,
                                prev_tail[q * SUBLANES:(q + 1) * SUBLANES, :],
                                tail[q * SUBLANES:(q + 1) * SUBLANES, :]))
        for q in range(CONV_WIDTH - 1)
    ]
    tail_ref[...] = tail
    x_ext = jnp.concatenate(head + [x_rnn], axis=0)
    xr = cb_ref[...]
    for k in range(CONV_WIDTH):
        xr = xr + x_ext[k * SUBLANES:k * SUBLANES + tile, :] * cw_ref[k:k + 1, :]

    gv = _gelu(v_raw)
    mu = jnp.mean(gv, axis=-1, keepdims=True)
    vc = gv - mu
    var = jnp.mean(vc * vc, axis=-1, keepdims=True)
    vln = (vc * lax.rsqrt(var + EPS) * lng_ref[...] + lnb_ref[...]).astype(BF16)
    u_raw = proj(OFF_U, D_SGU)

    piece = GATE_BLOCK
    gate_offsets = [OFF_GATE_A + c for c in range(0, 2 * D_MODEL, piece)]
    per_block = -(-len(gate_offsets) // N_GATE_BLOCKS)
    gate_pieces = []

    neg_lam = -lam_ref[...]
    softplus = jnp.maximum(neg_lam, 0.0) + jnp.log1p(jnp.exp(-jnp.abs(neg_lam)))
    decay_rate = LRU_C * softplus
    row_b = lax.broadcasted_iota(jnp.int32, (SUBLANES, GATE_BLOCK), 0)
    ya_blocks = []
    for blk in range(N_GATE_BLOCKS):
        lo = blk * GATE_BLOCK
        xr_b = xr[:, lo:lo + GATE_BLOCK]
        z = _dot(xr_b.astype(BF16), wg_ref[blk]) + bg_ref[blk]
        g_rnn = proj_tm(OFF_GRNN + lo, GATE_BLOCK)
        for off in gate_offsets[blk * per_block:(blk + 1) * per_block]:
            gate_pieces.append(proj(off, piece))
        r = jax.nn.sigmoid(z[:, :GATE_BLOCK])
        i = jax.nn.sigmoid(z[:, GATE_BLOCK:])
        neg_log_a = r * decay_rate[:, lo:lo + GATE_BLOCK]
        a = jnp.exp(-neg_log_a)
        norm = _sqrt_nonneg(jnp.tanh(neg_log_a) * (1.0 + a * a))
        u = norm * (i * xr_b)

        hl = u[0:SUBLANES, :]
        pr = a[0:SUBLANES, :]
        local_states = [hl]
        decays = [pr]
        for s in range(1, steps):
            a_s = a[s * SUBLANES:(s + 1) * SUBLANES, :]
            hl = a_s * hl + u[s * SUBLANES:(s + 1) * SUBLANES, :]
            pr = pr * a_s
            local_states.append(hl)
            decays.append(pr)
        for k in (1, 2, 4):
            keep = row_b >= k
            p_prev = jnp.where(keep, pltpu.roll(pr, k, 0), 1.0)
            h_prev = jnp.where(keep, pltpu.roll(hl, k, 0), 0.0)
            hl = pr * h_prev + hl
            pr = pr * p_prev
        carry = hc_ref[:, lo:lo + GATE_BLOCK]
        ends = hl + pr * carry
        entry = jnp.where(row_b >= 1, _shift_chains(ends), carry)
        hc_ref[:, lo:lo + GATE_BLOCK] = jnp.broadcast_to(
            ends[SUBLANES - 1:SUBLANES, :], (SUBLANES, GATE_BLOCK))
        states = (jnp.concatenate(local_states, axis=0)
                  + jnp.concatenate(decays, axis=0) * jnp.concatenate([entry] * steps, axis=0))
        ya_blocks.append((states * _gelu(g_rnn)).astype(BF16))
    gate_a_raw = jnp.concatenate(gate_pieces[:D_MODEL // piece], axis=1)
    gate_b_raw = jnp.concatenate(gate_pieces[D_MODEL // piece:], axis=1)

    t_chunk = lax.broadcasted_iota(jnp.int32, (SGU_BLOCK, SGU_BLOCK), 0) // CHUNK
    s_chunk = lax.broadcasted_iota(jnp.int32, (SGU_BLOCK, SGU_BLOCK), 1) // CHUNK
    causal = t_chunk >= s_chunk
    ws = [jnp.where(causal, ws_ref[g], 0.0).astype(BF16) for g in range(SGU_GROUPS)]
    blocks = []
    for n in range(tile // SGU_BLOCK):
        rows = slice(n * SGU_BLOCK, (n + 1) * SGU_BLOCK)
        cols = [
            _dot(ws[g], vln[rows, g * SGU_GROUP_DIM:(g + 1) * SGU_GROUP_DIM])
            for g in range(SGU_GROUPS)
        ]
        blocks.append(jnp.concatenate(cols, axis=1) + bs_ref[...])
    mixed = jnp.concatenate(blocks, axis=0)

    ya_tok = _dot(unperm_ref[...], jnp.concatenate(ya_blocks, axis=1)).astype(BF16)
    ya = _dot(ya_tok, wa_ref[...])
    yb = _dot((_gelu(u_raw) * mixed).astype(BF16), wb_ref[...])

    merged = (jax.nn.sigmoid(gate_a_raw) * ya + jax.nn.sigmoid(gate_b_raw) * yb).astype(BF16)
    o_ref[...] = x + _dot(merged, wo_ref[...])


def _ffn_kernel(x_ref, ng_ref, wu_ref, wd_ref, fg_ref, o_ref, h_ref, *, final_norm):
    x = x_ref[...]
    h_ref[...] = _rmsnorm(x, ng_ref[...]).astype(BF16)
    acc = x
    for c in range(D_FF // FF_CHUNK):
        lo = c * FF_CHUNK
        f = jnp.maximum(_dot(h_ref[...], wu_ref[:, lo:lo + FF_CHUNK]), 0.0)
        acc = acc + _dot((f * f).astype(BF16), wd_ref[lo:lo + FF_CHUNK, :])
    if final_norm:
        acc = _rmsnorm(acc, fg_ref[...])
    o_ref[...] = acc


class _Layer:
    def __init__(self, stacked, layer):
        self.stacked = stacked
        self.layer = layer


def _resident_spec(p):
    if isinstance(p, _Layer):
        shape = p.stacked.shape[1:]
        layer = p.layer
        return pl.BlockSpec((None,) + shape, lambda i: (layer,) + (0,) * len(shape),
                            pipeline_mode=pl.Buffered(1))
    ndim = p.ndim
    return pl.BlockSpec(p.shape, lambda i: (0,) * ndim, pipeline_mode=pl.Buffered(1))


def _operand(p):
    return p.stacked if isinstance(p, _Layer) else p


def _time_major_permutation(tile):
    steps = tile // SUBLANES
    dst = np.arange(tile)
    src = (dst % SUBLANES) * steps + dst // SUBLANES
    perm = np.zeros((tile, tile), np.float32)
    perm[dst, src] = 1.0
    return perm


def _mixer_call(x2, params, *, seq_len):
    n_tok = x2.shape[0]
    tile = MIXER_TILE
    perm = _time_major_permutation(tile)
    params = (jnp.asarray(perm, BF16), jnp.asarray(perm.T, BF16)) + tuple(params)
    kern = functools.partial(_mixer_kernel, tile=tile, tiles_per_seq=seq_len // tile)
    tok_spec = pl.BlockSpec((tile, D_MODEL), lambda i: (i, 0))
    return pl.pallas_call(
        kern,
        grid=(n_tok // tile,),
        in_specs=[tok_spec] + [_resident_spec(p) for p in params],
        out_specs=tok_spec,
        out_shape=jax.ShapeDtypeStruct((n_tok, D_MODEL), F32),
        scratch_shapes=[
            pltpu.VMEM((tile, D_MODEL), BF16),
            pltpu.VMEM((tile, D_MODEL), BF16),
            pltpu.VMEM((CONV_TAIL, D_RNN), F32),
            pltpu.VMEM((SUBLANES, D_RNN), F32),
        ],
        compiler_params=pltpu.CompilerParams(
            dimension_semantics=("arbitrary",), vmem_limit_bytes=VMEM_LIMIT_BYTES),
        name="mixer",
    )(x2, *[_operand(p) for p in params])


def _ffn_call(x2, params, *, final_norm):
    n_tok = x2.shape[0]
    tile = FFN_TILE
    kern = functools.partial(_ffn_kernel, final_norm=final_norm)
    tok_spec = pl.BlockSpec((tile, D_MODEL), lambda i: (i, 0))
    return pl.pallas_call(
        kern,
        grid=(n_tok // tile,),
        in_specs=[tok_spec] + [_resident_spec(p) for p in params],
        out_specs=tok_spec,
        out_shape=jax.ShapeDtypeStruct((n_tok, D_MODEL), F32),
        scratch_shapes=[pltpu.VMEM((tile, D_MODEL), BF16)],
        compiler_params=pltpu.CompilerParams(
            dimension_semantics=("arbitrary",), vmem_limit_bytes=VMEM_LIMIT_BYTES),
        name="ffn",
    )(x2, *[_operand(p) for p in params])


def _gate_weights(w_a, w_x, b_a, b_x):
    eye = jnp.eye(HEADS_PER_GATE_BLOCK, dtype=w_a.dtype)

    def block_diag(w):
        w = w.reshape(N_GATE_BLOCKS, HEADS_PER_GATE_BLOCK, RNN_HEAD_DIM, RNN_HEAD_DIM)
        return jnp.einsum('jhik,hg->jhigk', w, eye).reshape(N_GATE_BLOCKS, GATE_BLOCK, GATE_BLOCK)

    wg = jnp.concatenate([block_diag(w_a), block_diag(w_x)], axis=-1).astype(BF16)
    bg = jnp.concatenate([b_a.reshape(N_GATE_BLOCKS, 1, GATE_BLOCK),
                          b_x.reshape(N_GATE_BLOCKS, 1, GATE_BLOCK)], axis=-1)
    return wg, bg


def kernel(x, norm_mix_g, w_in, conv_w, conv_b, lru_w_a, lru_b_a, lru_w_x, lru_b_x, lru_lambda, sgu_ln_g, sgu_ln_b, sgu_w_s, sgu_b_s, w_branch_a, w_branch_b, w_out, norm_ffn_g, w_up, w_down, final_norm_g):
    batch, seq_len, _ = x.shape
    depth = w_in.shape[0]
    assert seq_len % MIXER_TILE == 0 and (batch * seq_len) % FFN_TILE == 0
    x2 = x.reshape(batch * seq_len, D_MODEL)
    w_in, w_branch_a, w_branch_b, w_out, w_up, w_down = (
        w.astype(BF16) for w in (w_in, w_branch_a, w_branch_b, w_out, w_up, w_down))
    for l in range(depth):
        wg, bg = _gate_weights(lru_w_a[l], lru_w_x[l], lru_b_a[l], lru_b_x[l])
        bs = jnp.repeat(sgu_b_s[l].T, SGU_GROUP_DIM, axis=1)
        mixer_params = (
            norm_mix_g[l].reshape(1, D_MODEL),
            _Layer(w_in, l),
            conv_w[l],
            conv_b[l].reshape(1, D_RNN),
            wg, bg,
            lru_lambda[l].reshape(1, D_RNN),
            sgu_ln_g[l].reshape(1, D_SGU),
            sgu_ln_b[l].reshape(1, D_SGU),
            sgu_w_s[l],
            bs,
            _Layer(w_branch_a, l),
            _Layer(w_branch_b, l),
            _Layer(w_out, l),
        )
        x2 = _mixer_call(x2, mixer_params, seq_len=seq_len)
        ffn_params = (
            norm_ffn_g[l].reshape(1, D_MODEL),
            _Layer(w_up, l),
            _Layer(w_down, l),
            final_norm_g.reshape(1, D_MODEL),
        )
        x2 = _ffn_call(x2, ffn_params, final_norm=(l == depth - 1))
    return x2.reshape(batch, seq_len, D_MODEL)
```

```python
import functools

import numpy as np
import jax
import jax.numpy as jnp
from jax import lax
from jax.experimental import pallas as pl
from jax.experimental.pallas import tpu as pltpu

D_MODEL = 1024
D_RNN = 1280
RNN_HEADS = 20
RNN_HEAD_DIM = D_RNN // RNN_HEADS
CONV_WIDTH = 4
LRU_C = 8.0
D_SGU = 1024
SGU_GROUPS = 8
SGU_GROUP_DIM = D_SGU // SGU_GROUPS
SGU_BLOCK = 128
CHUNK = 64
D_FF = 4 * D_MODEL
EPS = 1e-6

OFF_XRNN = 0
OFF_GRNN = OFF_XRNN + D_RNN
OFF_U = OFF_GRNN + D_RNN
OFF_V = OFF_U + D_SGU
OFF_GATE_A = OFF_V + D_SGU
OFF_GATE_B = OFF_GATE_A + D_MODEL
D_IN = OFF_GATE_B + D_MODEL

SUBLANES = 8
GATE_BLOCK = 256
N_GATE_BLOCKS = D_RNN // GATE_BLOCK
HEADS_PER_GATE_BLOCK = GATE_BLOCK // RNN_HEAD_DIM
CONV_TAIL = (CONV_WIDTH - 1) * SUBLANES
FF_CHUNK = 1024

MIXER_TILE = 256
FFN_TILE = 512
VMEM_LIMIT_BYTES = 56 * 1024 * 1024

F32 = jnp.float32
BF16 = jnp.bfloat16


def _dot(a, b):
    return jnp.dot(a, b, preferred_element_type=F32)


def _rmsnorm(x, g):
    return x * lax.rsqrt(jnp.mean(x * x, axis=-1, keepdims=True) + EPS) * g


def _gelu(x):
    c = 2.0 * 0.7978845608028654
    return x * jax.nn.sigmoid(x * (c + (c * 0.044715) * (x * x)))


def _sqrt_nonneg(y):
    return jnp.where(y > 0.0, y * lax.rsqrt(y), 0.0)


def _shift_chains(v):
    return pltpu.roll(v, 1, 0)


def _mixer_kernel(x_ref, xn_ref, perm_ref, unperm_ref, ng_ref, win_ref, cw_ref, cb_ref, wg_ref, bg_ref,
                  lam_ref, lng_ref, lnb_ref, ws_ref, bs_ref, wa_ref, wb_ref, wo_ref,
                  o_ref, h_ref, hp_ref, hn_ref, hpn_ref, tail_ref, hc_ref, *, tile, tiles_per_seq):
    steps = tile // SUBLANES
    seq_tile = pl.program_id(0) % tiles_per_seq

    @pl.when(seq_tile == 0)
    def _():
        tail_ref[...] = jnp.zeros((CONV_TAIL, D_RNN), F32)
        hc_ref[...] = jnp.zeros((SUBLANES, D_RNN), F32)

    def normalise(src_ref):
        hn_ref[...] = _rmsnorm(src_ref[...], ng_ref[...]).astype(BF16)
        hpn_ref[...] = _dot(perm_ref[...], hn_ref[...]).astype(BF16)

    @pl.when(pl.program_id(0) == 0)
    def _():
        normalise(x_ref)

    h_ref[...] = hn_ref[...]
    hp_ref[...] = hpn_ref[...]
    x = x_ref[...]

    def proj(lo, width):
        return _dot(h_ref[...], win_ref[:, lo:lo + width])

    def proj_tm(lo, width):
        return _dot(hp_ref[...], win_ref[:, lo:lo + width])

    x_rnn = proj_tm(OFF_XRNN, D_RNN)
    v_raw = proj(OFF_V, D_SGU)

    tail = x_rnn[tile - CONV_TAIL:, :]
    row = lax.broadcasted_iota(jnp.int32, (SUBLANES, D_RNN), 0)
    prev_tail = tail_ref[...]
    head = [
        _shift_chains(jnp.where(row == SUBLANES - 1,
                                prev_tail[q * SUBLANES:(q + 1) * SUBLANES, :],
                                tail[q * SUBLANES:(q + 1) * SUBLANES, :]))
        for q in range(CONV_WIDTH - 1)
    ]
    tail_ref[...] = tail
    x_ext = jnp.concatenate(head + [x_rnn], axis=0)
    xr = cb_ref[...]
    for k in range(CONV_WIDTH):
        xr = xr + x_ext[k * SUBLANES:k * SUBLANES + tile, :] * cw_ref[k:k + 1, :]

    gv = _gelu(v_raw)
    mu = jnp.mean(gv, axis=-1, keepdims=True)
    vc = gv - mu
    var = jnp.mean(vc * vc, axis=-1, keepdims=True)
    vln = (vc * lax.rsqrt(var + EPS) * lng_ref[...] + lnb_ref[...]).astype(BF16)
    u_raw = proj(OFF_U, D_SGU)

    piece = GATE_BLOCK
    gate_offsets = [OFF_GATE_A + c for c in range(0, 2 * D_MODEL, piece)]
    per_block = -(-len(gate_offsets) // N_GATE_BLOCKS)
    gate_pieces = []

    neg_lam = -lam_ref[...]
    softplus = jnp.maximum(neg_lam, 0.0) + jnp.log1p(jnp.exp(-jnp.abs(neg_lam)))
    decay_rate = LRU_C * softplus
    row_b = lax.broadcasted_iota(jnp.int32, (SUBLANES, GATE_BLOCK), 0)
    ya_blocks = []
    for blk in range(N_GATE_BLOCKS):
        lo = blk * GATE_BLOCK
        xr_b = xr[:, lo:lo + GATE_BLOCK]
        z = _dot(xr_b.astype(BF16), wg_ref[blk]) + bg_ref[blk]
        g_rnn = proj_tm(OFF_GRNN + lo, GATE_BLOCK)
        for off in gate_offsets[blk * per_block:(blk + 1) * per_block]:
            gate_pieces.append(proj(off, piece))
        r = jax.nn.sigmoid(z[:, :GATE_BLOCK])
        i = jax.nn.sigmoid(z[:, GATE_BLOCK:])
        neg_log_a = r * decay_rate[:, lo:lo + GATE_BLOCK]
        a = jnp.exp(-neg_log_a)
        norm = _sqrt_nonneg(jnp.tanh(neg_log_a) * (1.0 + a * a))
        u = norm * (i * xr_b)

        hl = u[0:SUBLANES, :]
        pr = a[0:SUBLANES, :]
        local_states = [hl]
        decays = [pr]
        for s in range(1, steps):
            a_s = a[s * SUBLANES:(s + 1) * SUBLANES, :]
            hl = a_s * hl + u[s * SUBLANES:(s + 1) * SUBLANES, :]
            pr = pr * a_s
            local_states.append(hl)
            decays.append(pr)
        for k in (1, 2, 4):
            keep = row_b >= k
            p_prev = jnp.where(keep, pltpu.roll(pr, k, 0), 1.0)
            h_prev = jnp.where(keep, pltpu.roll(hl, k, 0), 0.0)
            hl = pr * h_prev + hl
            pr = pr * p_prev
        carry = hc_ref[:, lo:lo + GATE_BLOCK]
        ends = hl + pr * carry
        entry = jnp.where(row_b >= 1, _shift_chains(ends), carry)
        hc_ref[:, lo:lo + GATE_BLOCK] = jnp.broadcast_to(
            ends[SUBLANES - 1:SUBLANES, :], (SUBLANES, GATE_BLOCK))
        states = (jnp.concatenate(local_states, axis=0)
                  + jnp.concatenate(decays, axis=0) * jnp.concatenate([entry] * steps, axis=0))
        ya_blocks.append((states * _gelu(g_rnn)).astype(BF16))
    gate_a_raw = jnp.concatenate(gate_pieces[:D_MODEL // piece], axis=1)
    gate_b_raw = jnp.concatenate(gate_pieces[D_MODEL // piece:], axis=1)

    t_chunk = lax.broadcasted_iota(jnp.int32, (SGU_BLOCK, SGU_BLOCK), 0) // CHUNK
    s_chunk = lax.broadcasted_iota(jnp.int32, (SGU_BLOCK, SGU_BLOCK), 1) // CHUNK
    causal = t_chunk >= s_chunk
    ws = [jnp.where(causal, ws_ref[g], 0.0).astype(BF16) for g in range(SGU_GROUPS)]
    blocks = []
    for n in range(tile // SGU_BLOCK):
        rows = slice(n * SGU_BLOCK, (n + 1) * SGU_BLOCK)
        cols = [
            _dot(ws[g], vln[rows, g * SGU_GROUP_DIM:(g + 1) * SGU_GROUP_DIM])
            for g in range(SGU_GROUPS)
        ]
        blocks.append(jnp.concatenate(cols, axis=1) + bs_ref[...])
    mixed = jnp.concatenate(blocks, axis=0)

    ya_tok = _dot(unperm_ref[...], jnp.concatenate(ya_blocks, axis=1)).astype(BF16)
    ya = _dot(ya_tok, wa_ref[...])
    yb = _dot((_gelu(u_raw) * mixed).astype(BF16), wb_ref[...])
    normalise(xn_ref)

    merged = (jax.nn.sigmoid(gate_a_raw) * ya + jax.nn.sigmoid(gate_b_raw) * yb).astype(BF16)
    o_ref[...] = x + _dot(merged, wo_ref[...])


def _ffn_kernel(x_ref, xn_ref, ng_ref, wu_ref, wd_ref, fg_ref, o_ref, h_ref, hn_ref, fn_ref, *,
                final_norm):
    n_chunks = D_FF // FF_CHUNK

    def hidden(src_ref, c):
        f = jnp.maximum(_dot(src_ref[...], wu_ref[:, c * FF_CHUNK:(c + 1) * FF_CHUNK]), 0.0)
        return (f * f).astype(BF16)

    def start_tile(src_ref):
        hn_ref[...] = _rmsnorm(src_ref[...], ng_ref[...]).astype(BF16)
        fn_ref[...] = hidden(hn_ref, 0)

    @pl.when(pl.program_id(0) == 0)
    def _():
        start_tile(x_ref)

    h_ref[...] = hn_ref[...]
    x = x_ref[...]
    acc = x + _dot(fn_ref[...], wd_ref[0:FF_CHUNK, :])
    for c in range(1, n_chunks):
        f = hidden(h_ref, c)
        if c == n_chunks - 1:
            start_tile(xn_ref)
        acc = acc + _dot(f, wd_ref[c * FF_CHUNK:(c + 1) * FF_CHUNK, :])
    if final_norm:
        acc = _rmsnorm(acc, fg_ref[...])
    o_ref[...] = acc


class _Layer:
    def __init__(self, stacked, layer):
        self.stacked = stacked
        self.layer = layer


def _resident_spec(p):
    if isinstance(p, _Layer):
        shape = p.stacked.shape[1:]
        layer = p.layer
        return pl.BlockSpec((None,) + shape, lambda i: (layer,) + (0,) * len(shape),
                            pipeline_mode=pl.Buffered(1))
    ndim = p.ndim
    return pl.BlockSpec(p.shape, lambda i: (0,) * ndim, pipeline_mode=pl.Buffered(1))


def _operand(p):
    return p.stacked if isinstance(p, _Layer) else p


def _time_major_permutation(tile):
    steps = tile // SUBLANES
    dst = np.arange(tile)
    src = (dst % SUBLANES) * steps + dst // SUBLANES
    perm = np.zeros((tile, tile), np.float32)
    perm[dst, src] = 1.0
    return perm


def _mixer_call(x2, params, *, seq_len):
    n_tok = x2.shape[0]
    tile = MIXER_TILE
    perm = _time_major_permutation(tile)
    params = (jnp.asarray(perm, BF16), jnp.asarray(perm.T, BF16)) + tuple(params)
    kern = functools.partial(_mixer_kernel, tile=tile, tiles_per_seq=seq_len // tile)
    tok_spec = pl.BlockSpec((tile, D_MODEL), lambda i: (i, 0))
    last = n_tok // tile - 1
    next_spec = pl.BlockSpec((tile, D_MODEL), lambda i: (jnp.minimum(i + 1, last), 0))
    return pl.pallas_call(
        kern,
        grid=(n_tok // tile,),
        in_specs=[tok_spec, next_spec] + [_resident_spec(p) for p in params],
        out_specs=tok_spec,
        out_shape=jax.ShapeDtypeStruct((n_tok, D_MODEL), F32),
        scratch_shapes=[
            pltpu.VMEM((tile, D_MODEL), BF16),
            pltpu.VMEM((tile, D_MODEL), BF16),
            pltpu.VMEM((tile, D_MODEL), BF16),
            pltpu.VMEM((tile, D_MODEL), BF16),
            pltpu.VMEM((CONV_TAIL, D_RNN), F32),
            pltpu.VMEM((SUBLANES, D_RNN), F32),
        ],
        compiler_params=pltpu.CompilerParams(
            dimension_semantics=("arbitrary",), vmem_limit_bytes=VMEM_LIMIT_BYTES),
        name="mixer",
    )(x2, x2, *[_operand(p) for p in params])


def _ffn_call(x2, params, *, final_norm):
    n_tok = x2.shape[0]
    tile = FFN_TILE
    kern = functools.partial(_ffn_kernel, final_norm=final_norm)
    tok_spec = pl.BlockSpec((tile, D_MODEL), lambda i: (i, 0))
    last = n_tok // tile - 1
    next_spec = pl.BlockSpec((tile, D_MODEL), lambda i: (jnp.minimum(i + 1, last), 0))
    return pl.pallas_call(
        kern,
        grid=(n_tok // tile,),
        in_specs=[tok_spec, next_spec] + [_resident_spec(p) for p in params],
        out_specs=tok_spec,
        out_shape=jax.ShapeDtypeStruct((n_tok, D_MODEL), F32),
        scratch_shapes=[pltpu.VMEM((tile, D_MODEL), BF16),
                        pltpu.VMEM((tile, D_MODEL), BF16),
                        pltpu.VMEM((tile, FF_CHUNK), BF16)],
        compiler_params=pltpu.CompilerParams(
            dimension_semantics=("arbitrary",), vmem_limit_bytes=VMEM_LIMIT_BYTES),
        name="ffn",
    )(x2, x2, *[_operand(p) for p in params])


def _gate_weights(w_a, w_x, b_a, b_x):
    eye = jnp.eye(HEADS_PER_GATE_BLOCK, dtype=w_a.dtype)

    def block_diag(w):
        w = w.reshape(N_GATE_BLOCKS, HEADS_PER_GATE_BLOCK, RNN_HEAD_DIM, RNN_HEAD_DIM)
        return jnp.einsum('jhik,hg->jhigk', w, eye).reshape(N_GATE_BLOCKS, GATE_BLOCK, GATE_BLOCK)

    wg = jnp.concatenate([block_diag(w_a), block_diag(w_x)], axis=-1).astype(BF16)
    bg = jnp.concatenate([b_a.reshape(N_GATE_BLOCKS, 1, GATE_BLOCK),
                          b_x.reshape(N_GATE_BLOCKS, 1, GATE_BLOCK)], axis=-1)
    return wg, bg


def kernel(x, norm_mix_g, w_in, conv_w, conv_b, lru_w_a, lru_b_a, lru_w_x, lru_b_x, lru_lambda, sgu_ln_g, sgu_ln_b, sgu_w_s, sgu_b_s, w_branch_a, w_branch_b, w_out, norm_ffn_g, w_up, w_down, final_norm_g):
    batch, seq_len, _ = x.shape
    depth = w_in.shape[0]
    assert seq_len % MIXER_TILE == 0 and (batch * seq_len) % FFN_TILE == 0
    x2 = x.reshape(batch * seq_len, D_MODEL)
    w_in, w_branch_a, w_branch_b, w_out, w_up, w_down = (
        w.astype(BF16) for w in (w_in, w_branch_a, w_branch_b, w_out, w_up, w_down))
    for l in range(depth):
        wg, bg = _gate_weights(lru_w_a[l], lru_w_x[l], lru_b_a[l], lru_b_x[l])
        bs = jnp.repeat(sgu_b_s[l].T, SGU_GROUP_DIM, axis=1)
        mixer_params = (
            norm_mix_g[l].reshape(1, D_MODEL),
            _Layer(w_in, l),
            conv_w[l],
            conv_b[l].reshape(1, D_RNN),
            wg, bg,
            lru_lambda[l].reshape(1, D_RNN),
            sgu_ln_g[l].reshape(1, D_SGU),
            sgu_ln_b[l].reshape(1, D_SGU),
            sgu_w_s[l],
            bs,
            _Layer(w_branch_a, l),
            _Layer(w_branch_b, l),
            _Layer(w_out, l),
        )
        x2 = _mixer_call(x2, mixer_params, seq_len=seq_len)
        ffn_params = (
            norm_ffn_g[l].reshape(1, D_MODEL),
            _Layer(w_up, l),
            _Layer(w_down, l),
            final_norm_g.reshape(1, D_MODEL),
        )
        x2 = _ffn_call(x2, ffn_params, final_norm=(l == depth - 1))
    return x2.reshape(batch, seq_len, D_MODEL)
```

```python
import functools

import numpy as np
import jax
import jax.numpy as jnp
from jax import lax
from jax.experimental import pallas as pl
from jax.experimental.pallas import tpu as pltpu

D_MODEL = 1024
D_RNN = 1280
RNN_HEADS = 20
RNN_HEAD_DIM = D_RNN // RNN_HEADS
CONV_WIDTH = 4
LRU_C = 8.0
D_SGU = 1024
SGU_GROUPS = 8
SGU_GROUP_DIM = D_SGU // SGU_GROUPS
SGU_BLOCK = 128
CHUNK = 64
D_FF = 4 * D_MODEL
EPS = 1e-6

OFF_XRNN = 0
OFF_GRNN = OFF_XRNN + D_RNN
OFF_U = OFF_GRNN + D_RNN
OFF_V = OFF_U + D_SGU
OFF_GATE_A = OFF_V + D_SGU
OFF_GATE_B = OFF_GATE_A + D_MODEL
D_IN = OFF_GATE_B + D_MODEL

SUBLANES = 8
GATE_BLOCK = 256
N_GATE_BLOCKS = D_RNN // GATE_BLOCK
HEADS_PER_GATE_BLOCK = GATE_BLOCK // RNN_HEAD_DIM
CONV_TAIL = (CONV_WIDTH - 1) * SUBLANES
FF_CHUNK = 1024

MIXER_TILE = 256
FFN_TILE = 1024
VMEM_LIMIT_BYTES = 56 * 1024 * 1024

F32 = jnp.float32
BF16 = jnp.bfloat16


def _dot(a, b):
    return jnp.dot(a, b, preferred_element_type=F32)


def _rmsnorm(x, g):
    return x * lax.rsqrt(jnp.mean(x * x, axis=-1, keepdims=True) + EPS) * g


def _gelu(x):
    c = 2.0 * 0.7978845608028654
    return x * jax.nn.sigmoid(x * (c + (c * 0.044715) * (x * x)))


def _sqrt_nonneg(y):
    return jnp.where(y > 0.0, y * lax.rsqrt(y), 0.0)


def _shift_chains(v):
    return pltpu.roll(v, 1, 0)


def _mixer_kernel(x_ref, perm_ref, unperm_ref, ng_ref, win_ref, cw_ref, cb_ref, wg_ref, bg_ref,
                  lam_ref, lng_ref, lnb_ref, ws_ref, bs_ref, wa_ref, wb_ref, wo_ref,
                  o_ref, h_ref, hp_ref, tail_ref, hc_ref, *, tile, tiles_per_seq):
    steps = tile // SUBLANES
    seq_tile = pl.program_id(0) % tiles_per_seq

    @pl.when(seq_tile == 0)
    def _():
        tail_ref[...] = jnp.zeros((CONV_TAIL, D_RNN), F32)
        hc_ref[...] = jnp.zeros((SUBLANES, D_RNN), F32)

    x = x_ref[...]
    h_ref[...] = _rmsnorm(x, ng_ref[...]).astype(BF16)
    hp_ref[...] = _dot(perm_ref[...], h_ref[...]).astype(BF16)

    def proj(lo, width):
        return _dot(h_ref[...], win_ref[:, lo:lo + width])

    def proj_tm(lo, width):
        return _dot(hp_ref[...], win_ref[:, lo:lo + width])

    x_rnn = proj_tm(OFF_XRNN, D_RNN)
    v_raw = proj(OFF_V, D_SGU)

    tail = x_rnn[tile - CONV_TAIL:, :]
    row = lax.broadcasted_iota(jnp.int32, (SUBLANES, D_RNN), 0)
    prev_tail = tail_ref[...]
    head = [
        _shift_chains(jnp.where(row == SUBLANES - 1,
                                prev_tail[q * SUBLANES:(q + 1) * SUBLANES, :],
                                tail[q * SUBLANES:(q + 1) * SUBLANES, :]))
        for q in range(CONV_WIDTH - 1)
    ]
    tail_ref[...] = tail
    x_ext = jnp.concatenate(head + [x_rnn], axis=0)
    xr = cb_ref[...]
    for k in range(CONV_WIDTH):
        xr = xr + x_ext[k * SUBLANES:k * SUBLANES + tile, :] * cw_ref[k:k + 1, :]

    gv = _gelu(v_raw)
    mu = jnp.mean(gv, axis=-1, keepdims=True)
    vc = gv - mu
    var = jnp.mean(vc * vc, axis=-1, keepdims=True)
    vln = (vc * lax.rsqrt(var + EPS) * lng_ref[...] + lnb_ref[...]).astype(BF16)
    u_raw = proj(OFF_U, D_SGU)

    piece = GATE_BLOCK
    gate_offsets = [OFF_GATE_A + c for c in range(0, 2 * D_MODEL, piece)]
    per_block = -(-len(gate_offsets) // N_GATE_BLOCKS)
    gate_pieces = []

    neg_lam = -lam_ref[...]
    softplus = jnp.maximum(neg_lam, 0.0) + jnp.log1p(jnp.exp(-jnp.abs(neg_lam)))
    decay_rate = LRU_C * softplus
    row_b = lax.broadcasted_iota(jnp.int32, (SUBLANES, GATE_BLOCK), 0)
    ya_blocks = []
    for blk in range(N_GATE_BLOCKS):
        lo = blk * GATE_BLOCK
        xr_b = xr[:, lo:lo + GATE_BLOCK]
        z = _dot(xr_b.astype(BF16), wg_ref[blk]) + bg_ref[blk]
        g_rnn = proj_tm(OFF_GRNN + lo, GATE_BLOCK)
        for off in gate_offsets[blk * per_block:(blk + 1) * per_block]:
            gate_pieces.append(proj(off, piece))
        r = jax.nn.sigmoid(z[:, :GATE_BLOCK])
        i = jax.nn.sigmoid(z[:, GATE_BLOCK:])
        neg_log_a = r * decay_rate[:, lo:lo + GATE_BLOCK]
        a = jnp.exp(-neg_log_a)
        norm = _sqrt_nonneg(jnp.tanh(neg_log_a) * (1.0 + a * a))
        u = norm * (i * xr_b)

        hl = u[0:SUBLANES, :]
        pr = a[0:SUBLANES, :]
        local_states = [hl]
        decays = [pr]
        for s in range(1, steps):
            a_s = a[s * SUBLANES:(s + 1) * SUBLANES, :]
            hl = a_s * hl + u[s * SUBLANES:(s + 1) * SUBLANES, :]
            pr = pr * a_s
            local_states.append(hl)
            decays.append(pr)
        for k in (1, 2, 4):
            keep = row_b >= k
            p_prev = jnp.where(keep, pltpu.roll(pr, k, 0), 1.0)
            h_prev = jnp.where(keep, pltpu.roll(hl, k, 0), 0.0)
            hl = pr * h_prev + hl
            pr = pr * p_prev
        carry = hc_ref[:, lo:lo + GATE_BLOCK]
        ends = hl + pr * carry
        entry = jnp.where(row_b >= 1, _shift_chains(ends), carry)
        hc_ref[:, lo:lo + GATE_BLOCK] = jnp.broadcast_to(
            ends[SUBLANES - 1:SUBLANES, :], (SUBLANES, GATE_BLOCK))
        states = (jnp.concatenate(local_states, axis=0)
                  + jnp.concatenate(decays, axis=0) * jnp.concatenate([entry] * steps, axis=0))
        ya_blocks.append((states * _gelu(g_rnn)).astype(BF16))
    gate_a_raw = jnp.concatenate(gate_pieces[:D_MODEL // piece], axis=1)
    gate_b_raw = jnp.concatenate(gate_pieces[D_MODEL // piece:], axis=1)

    t_chunk = lax.broadcasted_iota(jnp.int32, (SGU_BLOCK, SGU_BLOCK), 0) // CHUNK
    s_chunk = lax.broadcasted_iota(jnp.int32, (SGU_BLOCK, SGU_BLOCK), 1) // CHUNK
    causal = t_chunk >= s_chunk
    ws = [jnp.where(causal, ws_ref[g], 0.0).astype(BF16) for g in range(SGU_GROUPS)]
    blocks = []
    for n in range(tile // SGU_BLOCK):
        rows = slice(n * SGU_BLOCK, (n + 1) * SGU_BLOCK)
        cols = [
            _dot(ws[g], vln[rows, g * SGU_GROUP_DIM:(g + 1) * SGU_GROUP_DIM])
            for g in range(SGU_GROUPS)
        ]
        blocks.append(jnp.concatenate(cols, axis=1) + bs_ref[...])
    mixed = jnp.concatenate(blocks, axis=0)

    ya_tok = _dot(unperm_ref[...], jnp.concatenate(ya_blocks, axis=1)).astype(BF16)
    ya = _dot(ya_tok, wa_ref[...])
    yb = _dot((_gelu(u_raw) * mixed).astype(BF16), wb_ref[...])

    merged = (jax.nn.sigmoid(gate_a_raw) * ya + jax.nn.sigmoid(gate_b_raw) * yb).astype(BF16)
    o_ref[...] = x + _dot(merged, wo_ref[...])


def _ffn_kernel(x_ref, ng_ref, wu_ref, wd_ref, fg_ref, o_ref, h_ref, *, final_norm):
    x = x_ref[...]
    h_ref[...] = _rmsnorm(x, ng_ref[...]).astype(BF16)
    acc = x
    for c in range(D_FF // FF_CHUNK):
        lo = c * FF_CHUNK
        f = jnp.maximum(_dot(h_ref[...], wu_ref[:, lo:lo + FF_CHUNK]), 0.0)
        acc = acc + _dot((f * f).astype(BF16), wd_ref[lo:lo + FF_CHUNK, :])
    if final_norm:
        acc = _rmsnorm(acc, fg_ref[...])
    o_ref[...] = acc


class _Layer:
    def __init__(self, stacked, layer):
        self.stacked = stacked
        self.layer = layer


def _resident_spec(p):
    if isinstance(p, _Layer):
        shape = p.stacked.shape[1:]
        layer = p.layer
        return pl.BlockSpec((None,) + shape, lambda i: (layer,) + (0,) * len(shape),
                            pipeline_mode=pl.Buffered(1))
    ndim = p.ndim
    return pl.BlockSpec(p.shape, lambda i: (0,) * ndim, pipeline_mode=pl.Buffered(1))


def _operand(p):
    return p.stacked if isinstance(p, _Layer) else p


def _time_major_permutation(tile):
    steps = tile // SUBLANES
    dst = np.arange(tile)
    src = (dst % SUBLANES) * steps + dst // SUBLANES
    perm = np.zeros((tile, tile), np.float32)
    perm[dst, src] = 1.0
    return perm


def _mixer_call(x2, params, *, seq_len):
    n_tok = x2.shape[0]
    tile = MIXER_TILE
    perm = _time_major_permutation(tile)
    params = (jnp.asarray(perm, BF16), jnp.asarray(perm.T, BF16)) + tuple(params)
    kern = functools.partial(_mixer_kernel, tile=tile, tiles_per_seq=seq_len // tile)
    tok_spec = pl.BlockSpec((tile, D_MODEL), lambda i: (i, 0))
    return pl.pallas_call(
        kern,
        grid=(n_tok // tile,),
        in_specs=[tok_spec] + [_resident_spec(p) for p in params],
        out_specs=tok_spec,
        out_shape=jax.ShapeDtypeStruct((n_tok, D_MODEL), F32),
        scratch_shapes=[
            pltpu.VMEM((tile, D_MODEL), BF16),
            pltpu.VMEM((tile, D_MODEL), BF16),
            pltpu.VMEM((CONV_TAIL, D_RNN), F32),
            pltpu.VMEM((SUBLANES, D_RNN), F32),
        ],
        compiler_params=pltpu.CompilerParams(
            dimension_semantics=("arbitrary",), vmem_limit_bytes=VMEM_LIMIT_BYTES),
        name="mixer",
    )(x2, *[_operand(p) for p in params])


def _ffn_call(x2, params, *, final_norm):
    n_tok = x2.shape[0]
    tile = FFN_TILE
    kern = functools.partial(_ffn_kernel, final_norm=final_norm)
    tok_spec = pl.BlockSpec((tile, D_MODEL), lambda i: (i, 0))
    return pl.pallas_call(
        kern,
        grid=(n_tok // tile,),
        in_specs=[tok_spec] + [_resident_spec(p) for p in params],
        out_specs=tok_spec,
        out_shape=jax.ShapeDtypeStruct((n_tok, D_MODEL), F32),
        scratch_shapes=[pltpu.VMEM((tile, D_MODEL), BF16)],
        compiler_params=pltpu.CompilerParams(
            dimension_semantics=("arbitrary",), vmem_limit_bytes=VMEM_LIMIT_BYTES),
        name="ffn",
    )(x2, *[_operand(p) for p in params])


def _gate_weights(w_a, w_x, b_a, b_x):
    eye = jnp.eye(HEADS_PER_GATE_BLOCK, dtype=w_a.dtype)

    def block_diag(w):
        w = w.reshape(N_GATE_BLOCKS, HEADS_PER_GATE_BLOCK, RNN_HEAD_DIM, RNN_HEAD_DIM)
        return jnp.einsum('jhik,hg->jhigk', w, eye).reshape(N_GATE_BLOCKS, GATE_BLOCK, GATE_BLOCK)

    wg = jnp.concatenate([block_diag(w_a), block_diag(w_x)], axis=-1).astype(BF16)
    bg = jnp.concatenate([b_a.reshape(N_GATE_BLOCKS, 1, GATE_BLOCK),
                          b_x.reshape(N_GATE_BLOCKS, 1, GATE_BLOCK)], axis=-1)
    return wg, bg


def kernel(x, norm_mix_g, w_in, conv_w, conv_b, lru_w_a, lru_b_a, lru_w_x, lru_b_x, lru_lambda, sgu_ln_g, sgu_ln_b, sgu_w_s, sgu_b_s, w_branch_a, w_branch_b, w_out, norm_ffn_g, w_up, w_down, final_norm_g):
    batch, seq_len, _ = x.shape
    depth = w_in.shape[0]
    assert seq_len % MIXER_TILE == 0 and (batch * seq_len) % FFN_TILE == 0
    x2 = x.reshape(batch * seq_len, D_MODEL)
    w_in, w_branch_a, w_branch_b, w_out, w_up, w_down = (
        w.astype(BF16) for w in (w_in, w_branch_a, w_branch_b, w_out, w_up, w_down))
    for l in range(depth):
        wg, bg = _gate_weights(lru_w_a[l], lru_w_x[l], lru_b_a[l], lru_b_x[l])
        bs = jnp.repeat(sgu_b_s[l].T, SGU_GROUP_DIM, axis=1)
        mixer_params = (
            norm_mix_g[l].reshape(1, D_MODEL),
            _Layer(w_in, l),
            conv_w[l],
            conv_b[l].reshape(1, D_RNN),
            wg, bg,
            lru_lambda[l].reshape(1, D_RNN),
            sgu_ln_g[l].reshape(1, D_SGU),
            sgu_ln_b[l].reshape(1, D_SGU),
            sgu_w_s[l],
            bs,
            _Layer(w_branch_a, l),
            _Layer(w_branch_b, l),
            _Layer(w_out, l),
        )
        x2 = _mixer_call(x2, mixer_params, seq_len=seq_len)
        ffn_params = (
            norm_ffn_g[l].reshape(1, D_MODEL),
            _Layer(w_up, l),
            _Layer(w_down, l),
            final_norm_g.reshape(1, D_MODEL),
        )
        x2 = _ffn_call(x2, ffn_params, final_norm=(l == depth - 1))
    return x2.reshape(batch, seq_len, D_MODEL)
```

```python
import functools

import numpy as np
import jax
import jax.numpy as jnp
from jax import lax
from jax.experimental import pallas as pl
from jax.experimental.pallas import tpu as pltpu

D_MODEL = 1024
D_RNN = 1280
RNN_HEADS = 20
RNN_HEAD_DIM = D_RNN // RNN_HEADS
CONV_WIDTH = 4
LRU_C = 8.0
D_SGU = 1024
SGU_GROUPS = 8
SGU_GROUP_DIM = D_SGU // SGU_GROUPS
SGU_BLOCK = 128
CHUNK = 64
D_FF = 4 * D_MODEL
EPS = 1e-6

OFF_XRNN = 0
OFF_GRNN = OFF_XRNN + D_RNN
OFF_U = OFF_GRNN + D_RNN
OFF_V = OFF_U + D_SGU
OFF_GATE_A = OFF_V + D_SGU
OFF_GATE_B = OFF_GATE_A + D_MODEL
D_IN = OFF_GATE_B + D_MODEL

SUBLANES = 8
GATE_BLOCK = 256
N_GATE_BLOCKS = D_RNN // GATE_BLOCK
HEADS_PER_GATE_BLOCK = GATE_BLOCK // RNN_HEAD_DIM
CONV_TAIL = (CONV_WIDTH - 1) * SUBLANES
FF_CHUNK = 1024

MIXER_TILE = 256
FFN_TILE = 1024
VMEM_LIMIT_BYTES = 56 * 1024 * 1024

F32 = jnp.float32
BF16 = jnp.bfloat16


def _dot(a, b):
    return jnp.dot(a, b, preferred_element_type=F32)


def _rmsnorm(x, g):
    return x * lax.rsqrt(jnp.mean(x * x, axis=-1, keepdims=True) + EPS) * g


def _gelu(x):
    c = -2.0 * 0.7978845608028654 * 1.4426950408889634
    return x / (1.0 + jnp.exp2(x * (c + (c * 0.044715) * (x * x))))


def _sqrt_nonneg(y):
    return jnp.where(y > 0.0, y * lax.rsqrt(y), 0.0)


def _shift_chains(v):
    return pltpu.roll(v, 1, 0)


def _mixer_kernel(x_ref, perm_ref, unperm_ref, ng_ref, win_ref, cw_ref, cb_ref, wg_ref, bg_ref,
                  lam_ref, lng_ref, lnb_ref, ws_ref, bs_ref, wa_ref, wb_ref, wo_ref,
                  o_ref, h_ref, hp_ref, tail_ref, hc_ref, *, tile, tiles_per_seq):
    steps = tile // SUBLANES
    seq_tile = pl.program_id(0) % tiles_per_seq

    @pl.when(seq_tile == 0)
    def _():
        tail_ref[...] = jnp.zeros((CONV_TAIL, D_RNN), F32)
        hc_ref[...] = jnp.zeros((SUBLANES, D_RNN), F32)

    x = x_ref[...]
    h_ref[...] = _rmsnorm(x, ng_ref[...]).astype(BF16)
    hp_ref[...] = _dot(perm_ref[...], h_ref[...]).astype(BF16)

    def proj(lo, width):
        return _dot(h_ref[...], win_ref[:, lo:lo + width])

    def proj_tm(lo, width):
        return _dot(hp_ref[...], win_ref[:, lo:lo + width])

    x_rnn = proj_tm(OFF_XRNN, D_RNN)
    v_raw = proj(OFF_V, D_SGU)

    tail = x_rnn[tile - CONV_TAIL:, :]
    row = lax.broadcasted_iota(jnp.int32, (SUBLANES, D_RNN), 0)
    prev_tail = tail_ref[...]
    head = [
        _shift_chains(jnp.where(row == SUBLANES - 1,
                                prev_tail[q * SUBLANES:(q + 1) * SUBLANES, :],
                                tail[q * SUBLANES:(q + 1) * SUBLANES, :]))
        for q in range(CONV_WIDTH - 1)
    ]
    tail_ref[...] = tail
    x_ext = jnp.concatenate(head + [x_rnn], axis=0)
    xr = cb_ref[...]
    for k in range(CONV_WIDTH):
        xr = xr + x_ext[k * SUBLANES:k * SUBLANES + tile, :] * cw_ref[k:k + 1, :]

    gv = _gelu(v_raw)
    mu = jnp.mean(gv, axis=-1, keepdims=True)
    vc = gv - mu
    var = jnp.mean(vc * vc, axis=-1, keepdims=True)
    vln = (vc * lax.rsqrt(var + EPS) * lng_ref[...] + lnb_ref[...]).astype(BF16)
    u_raw = proj(OFF_U, D_SGU)

    piece = GATE_BLOCK
    gate_offsets = [OFF_GATE_A + c for c in range(0, 2 * D_MODEL, piece)]
    per_block = -(-len(gate_offsets) // N_GATE_BLOCKS)
    gate_pieces = []

    neg_lam = -lam_ref[...]
    softplus = jnp.maximum(neg_lam, 0.0) + jnp.log1p(jnp.exp(-jnp.abs(neg_lam)))
    decay_rate = LRU_C * softplus
    row_b = lax.broadcasted_iota(jnp.int32, (SUBLANES, GATE_BLOCK), 0)
    ya_blocks = []
    for blk in range(N_GATE_BLOCKS):
        lo = blk * GATE_BLOCK
        xr_b = xr[:, lo:lo + GATE_BLOCK]
        z = _dot(xr_b.astype(BF16), wg_ref[blk]) + bg_ref[blk]
        g_rnn = proj_tm(OFF_GRNN + lo, GATE_BLOCK)
        for off in gate_offsets[blk * per_block:(blk + 1) * per_block]:
            gate_pieces.append(proj(off, piece))
        r = jax.nn.sigmoid(z[:, :GATE_BLOCK])
        i = jax.nn.sigmoid(z[:, GATE_BLOCK:])
        neg_log_a = r * decay_rate[:, lo:lo + GATE_BLOCK]
        a = jnp.exp(-neg_log_a)
        norm = _sqrt_nonneg(jnp.tanh(neg_log_a) * (1.0 + a * a))
        u = norm * (i * xr_b)

        hl = u[0:SUBLANES, :]
        pr = a[0:SUBLANES, :]
        local_states = [hl]
        decays = [pr]
        for s in range(1, steps):
            a_s = a[s * SUBLANES:(s + 1) * SUBLANES, :]
            hl = a_s * hl + u[s * SUBLANES:(s + 1) * SUBLANES, :]
            pr = pr * a_s
            local_states.append(hl)
            decays.append(pr)
        for k in (1, 2, 4):
            keep = row_b >= k
            p_prev = jnp.where(keep, pltpu.roll(pr, k, 0), 1.0)
            h_prev = jnp.where(keep, pltpu.roll(hl, k, 0), 0.0)
            hl = pr * h_prev + hl
            pr = pr * p_prev
        carry = hc_ref[:, lo:lo + GATE_BLOCK]
        ends = hl + pr * carry
        entry = jnp.where(row_b >= 1, _shift_chains(ends), carry)
        hc_ref[:, lo:lo + GATE_BLOCK] = jnp.broadcast_to(
            ends[SUBLANES - 1:SUBLANES, :], (SUBLANES, GATE_BLOCK))
        states = (jnp.concatenate(local_states, axis=0)
                  + jnp.concatenate(decays, axis=0) * jnp.concatenate([entry] * steps, axis=0))
        ya_blocks.append((states * _gelu(g_rnn)).astype(BF16))
    gate_a_raw = jnp.concatenate(gate_pieces[:D_MODEL // piece], axis=1)
    gate_b_raw = jnp.concatenate(gate_pieces[D_MODEL // piece:], axis=1)

    t_chunk = lax.broadcasted_iota(jnp.int32, (SGU_BLOCK, SGU_BLOCK), 0) // CHUNK
    s_chunk = lax.broadcasted_iota(jnp.int32, (SGU_BLOCK, SGU_BLOCK), 1) // CHUNK
    causal = t_chunk >= s_chunk
    ws = [jnp.where(causal, ws_ref[g], 0.0).astype(BF16) for g in range(SGU_GROUPS)]
    blocks = []
    for n in range(tile // SGU_BLOCK):
        rows = slice(n * SGU_BLOCK, (n + 1) * SGU_BLOCK)
        cols = [
            _dot(ws[g], vln[rows, g * SGU_GROUP_DIM:(g + 1) * SGU_GROUP_DIM])
            for g in range(SGU_GROUPS)
        ]
        blocks.append(jnp.concatenate(cols, axis=1) + bs_ref[...])
    mixed = jnp.concatenate(blocks, axis=0)

    ya_tok = _dot(unperm_ref[...], jnp.concatenate(ya_blocks, axis=1)).astype(BF16)
    ya = _dot(ya_tok, wa_ref[...])
    yb = _dot((_gelu(u_raw) * mixed).astype(BF16), wb_ref[...])

    merged = (jax.nn.sigmoid(gate_a_raw) * ya + jax.nn.sigmoid(gate_b_raw) * yb).astype(BF16)
    o_ref[...] = x + _dot(merged, wo_ref[...])


def _ffn_kernel(x_ref, ng_ref, wu_ref, wd_ref, fg_ref, o_ref, h_ref, *, final_norm):
    x = x_ref[...]
    h_ref[...] = _rmsnorm(x, ng_ref[...]).astype(BF16)
    acc = x
    for c in range(D_FF // FF_CHUNK):
        lo = c * FF_CHUNK
        f = jnp.maximum(_dot(h_ref[...], wu_ref[:, lo:lo + FF_CHUNK]), 0.0)
        acc = acc + _dot((f * f).astype(BF16), wd_ref[lo:lo + FF_CHUNK, :])
    if final_norm:
        acc = _rmsnorm(acc, fg_ref[...])
    o_ref[...] = acc


class _Layer:
    def __init__(self, stacked, layer):
        self.stacked = stacked
        self.layer = layer


def _resident_spec(p):
    if isinstance(p, _Layer):
        shape = p.stacked.shape[1:]
        layer = p.layer
        return pl.BlockSpec((None,) + shape, lambda i: (layer,) + (0,) * len(shape),
                            pipeline_mode=pl.Buffered(1))
    ndim = p.ndim
    return pl.BlockSpec(p.shape, lambda i: (0,) * ndim, pipeline_mode=pl.Buffered(1))


def _operand(p):
    return p.stacked if isinstance(p, _Layer) else p


def _time_major_permutation(tile):
    steps = tile // SUBLANES
    dst = np.arange(tile)
    src = (dst % SUBLANES) * steps + dst // SUBLANES
    perm = np.zeros((tile, tile), np.float32)
    perm[dst, src] = 1.0
    return perm


def _mixer_call(x2, params, *, seq_len):
    n_tok = x2.shape[0]
    tile = MIXER_TILE
    perm = _time_major_permutation(tile)
    params = (jnp.asarray(perm, BF16), jnp.asarray(perm.T, BF16)) + tuple(params)
    kern = functools.partial(_mixer_kernel, tile=tile, tiles_per_seq=seq_len // tile)
    tok_spec = pl.BlockSpec((tile, D_MODEL), lambda i: (i, 0))
    return pl.pallas_call(
        kern,
        grid=(n_tok // tile,),
        in_specs=[tok_spec] + [_resident_spec(p) for p in params],
        out_specs=tok_spec,
        out_shape=jax.ShapeDtypeStruct((n_tok, D_MODEL), F32),
        scratch_shapes=[
            pltpu.VMEM((tile, D_MODEL), BF16),
            pltpu.VMEM((tile, D_MODEL), BF16),
            pltpu.VMEM((CONV_TAIL, D_RNN), F32),
            pltpu.VMEM((SUBLANES, D_RNN), F32),
        ],
        compiler_params=pltpu.CompilerParams(
            dimension_semantics=("arbitrary",), vmem_limit_bytes=VMEM_LIMIT_BYTES),
        name="mixer",
    )(x2, *[_operand(p) for p in params])


def _ffn_call(x2, params, *, final_norm):
    n_tok = x2.shape[0]
    tile = FFN_TILE
    kern = functools.partial(_ffn_kernel, final_norm=final_norm)
    tok_spec = pl.BlockSpec((tile, D_MODEL), lambda i: (i, 0))
    return pl.pallas_call(
        kern,
        grid=(n_tok // tile,),
        in_specs=[tok_spec] + [_resident_spec(p) for p in params],
        out_specs=tok_spec,
        out_shape=jax.ShapeDtypeStruct((n_tok, D_MODEL), F32),
        scratch_shapes=[pltpu.VMEM((tile, D_MODEL), BF16)],
        compiler_params=pltpu.CompilerParams(
            dimension_semantics=("arbitrary",), vmem_limit_bytes=VMEM_LIMIT_BYTES),
        name="ffn",
    )(x2, *[_operand(p) for p in params])


def _gate_weights(w_a, w_x, b_a, b_x):
    eye = jnp.eye(HEADS_PER_GATE_BLOCK, dtype=w_a.dtype)

    def block_diag(w):
        w = w.reshape(N_GATE_BLOCKS, HEADS_PER_GATE_BLOCK, RNN_HEAD_DIM, RNN_HEAD_DIM)
        return jnp.einsum('jhik,hg->jhigk', w, eye).reshape(N_GATE_BLOCKS, GATE_BLOCK, GATE_BLOCK)

    wg = jnp.concatenate([block_diag(w_a), block_diag(w_x)], axis=-1).astype(BF16)
    bg = jnp.concatenate([b_a.reshape(N_GATE_BLOCKS, 1, GATE_BLOCK),
                          b_x.reshape(N_GATE_BLOCKS, 1, GATE_BLOCK)], axis=-1)
    return wg, bg


def kernel(x, norm_mix_g, w_in, conv_w, conv_b, lru_w_a, lru_b_a, lru_w_x, lru_b_x, lru_lambda, sgu_ln_g, sgu_ln_b, sgu_w_s, sgu_b_s, w_branch_a, w_branch_b, w_out, norm_ffn_g, w_up, w_down, final_norm_g):
    batch, seq_len, _ = x.shape
    depth = w_in.shape[0]
    assert seq_len % MIXER_TILE == 0 and (batch * seq_len) % FFN_TILE == 0
    x2 = x.reshape(batch * seq_len, D_MODEL)
    w_in, w_branch_a, w_branch_b, w_out, w_up, w_down = (
        w.astype(BF16) for w in (w_in, w_branch_a, w_branch_b, w_out, w_up, w_down))
    for l in range(depth):
        wg, bg = _gate_weights(lru_w_a[l], lru_w_x[l], lru_b_a[l], lru_b_x[l])
        bs = jnp.repeat(sgu_b_s[l].T, SGU_GROUP_DIM, axis=1)
        mixer_params = (
            norm_mix_g[l].reshape(1, D_MODEL),
            _Layer(w_in, l),
            conv_w[l],
            conv_b[l].reshape(1, D_RNN),
            wg, bg,
            lru_lambda[l].reshape(1, D_RNN),
            sgu_ln_g[l].reshape(1, D_SGU),
            sgu_ln_b[l].reshape(1, D_SGU),
            sgu_w_s[l],
            bs,
            _Layer(w_branch_a, l),
            _Layer(w_branch_b, l),
            _Layer(w_out, l),
        )
        x2 = _mixer_call(x2, mixer_params, seq_len=seq_len)
        ffn_params = (
            norm_ffn_g[l].reshape(1, D_MODEL),
            _Layer(w_up, l),
            _Layer(w_down, l),
            final_norm_g.reshape(1, D_MODEL),
        )
        x2 = _ffn_call(x2, ffn_params, final_norm=(l == depth - 1))
    return x2.reshape(batch, seq_len, D_MODEL)
```

```python
import functools

import numpy as np
import jax
import jax.numpy as jnp
from jax import lax
from jax.experimental import pallas as pl
from jax.experimental.pallas import tpu as pltpu

D_MODEL = 1024
D_RNN = 1280
RNN_HEADS = 20
RNN_HEAD_DIM = D_RNN // RNN_HEADS
CONV_WIDTH = 4
LRU_C = 8.0
D_SGU = 1024
SGU_GROUPS = 8
SGU_GROUP_DIM = D_SGU // SGU_GROUPS
SGU_BLOCK = 128
CHUNK = 64
D_FF = 4 * D_MODEL
EPS = 1e-6

OFF_XRNN = 0
OFF_GRNN = OFF_XRNN + D_RNN
OFF_U = OFF_GRNN + D_RNN
OFF_V = OFF_U + D_SGU
OFF_GATE_A = OFF_V + D_SGU
OFF_GATE_B = OFF_GATE_A + D_MODEL
D_IN = OFF_GATE_B + D_MODEL

SUBLANES = 8
GATE_BLOCK = 256
N_GATE_BLOCKS = D_RNN // GATE_BLOCK
HEADS_PER_GATE_BLOCK = GATE_BLOCK // RNN_HEAD_DIM
CONV_TAIL = (CONV_WIDTH - 1) * SUBLANES
FF_CHUNK = 1024

MIXER_TILE = 256
FFN_TILE = 1024
VMEM_LIMIT_BYTES = 56 * 1024 * 1024

F32 = jnp.float32
BF16 = jnp.bfloat16


def _dot(a, b):
    return jnp.dot(a, b, preferred_element_type=F32)


def _rmsnorm(x, g):
    return x * lax.rsqrt(jnp.mean(x * x, axis=-1, keepdims=True) + EPS) * g


def _gelu(x):
    c = -2.0 * 0.7978845608028654 * 1.4426950408889634
    return x / (1.0 + jnp.exp2(x * (c + (c * 0.044715) * (x * x))))


def _sqrt_nonneg(y):
    return jnp.where(y > 0.0, y * lax.rsqrt(y), 0.0)


def _shift_chains(v):
    return pltpu.roll(v, 1, 0)


def _mixer_kernel(x_ref, perm_ref, unperm_ref, ng_ref, win_ref, cw_ref, cb_ref, wg_ref, bg_ref,
                  lam_ref, lng_ref, lnb_ref, ws_ref, bs_ref, wa_ref, wb_ref, wo_ref,
                  o_ref, h_ref, hp_ref, tail_ref, hc_ref, *, tile, tiles_per_seq):
    steps = tile // SUBLANES
    seq_tile = pl.program_id(0) % tiles_per_seq

    @pl.when(seq_tile == 0)
    def _():
        tail_ref[...] = jnp.zeros((CONV_TAIL, D_RNN), F32)
        hc_ref[...] = jnp.zeros((SUBLANES, D_RNN), F32)

    x = x_ref[...]
    hn = _rmsnorm(x, ng_ref[...])
    h_ref[...] = hn.astype(BF16)
    hp_ref[...] = pltpu.einshape(
        "srd->rsd", hn.reshape(SUBLANES, steps, D_MODEL)).reshape(tile, D_MODEL).astype(BF16)

    def proj(lo, width):
        return _dot(h_ref[...], win_ref[:, lo:lo + width])

    def proj_tm(lo, width):
        return _dot(hp_ref[...], win_ref[:, lo:lo + width])

    x_rnn = proj_tm(OFF_XRNN, D_RNN)
    v_raw = proj(OFF_V, D_SGU)

    tail = x_rnn[tile - CONV_TAIL:, :]
    row = lax.broadcasted_iota(jnp.int32, (SUBLANES, D_RNN), 0)
    prev_tail = tail_ref[...]
    head = [
        _shift_chains(jnp.where(row == SUBLANES - 1,
                                prev_tail[q * SUBLANES:(q + 1) * SUBLANES, :],
                                tail[q * SUBLANES:(q + 1) * SUBLANES, :]))
        for q in range(CONV_WIDTH - 1)
    ]
    tail_ref[...] = tail
    x_ext = jnp.concatenate(head + [x_rnn], axis=0)
    xr = cb_ref[...]
    for k in range(CONV_WIDTH):
        xr = xr + x_ext[k * SUBLANES:k * SUBLANES + tile, :] * cw_ref[k:k + 1, :]

    gv = _gelu(v_raw)
    mu = jnp.mean(gv, axis=-1, keepdims=True)
    vc = gv - mu
    var = jnp.mean(vc * vc, axis=-1, keepdims=True)
    vln = (vc * lax.rsqrt(var + EPS) * lng_ref[...] + lnb_ref[...]).astype(BF16)
    u_raw = proj(OFF_U, D_SGU)

    piece = GATE_BLOCK
    gate_offsets = [OFF_GATE_A + c for c in range(0, 2 * D_MODEL, piece)]
    per_block = -(-len(gate_offsets) // N_GATE_BLOCKS)
    gate_pieces = []

    neg_lam = -lam_ref[...]
    softplus = jnp.maximum(neg_lam, 0.0) + jnp.log1p(jnp.exp(-jnp.abs(neg_lam)))
    decay_rate = LRU_C * softplus
    row_b = lax.broadcasted_iota(jnp.int32, (SUBLANES, GATE_BLOCK), 0)
    ya_blocks = []
    for blk in range(N_GATE_BLOCKS):
        lo = blk * GATE_BLOCK
        xr_b = xr[:, lo:lo + GATE_BLOCK]
        z = _dot(xr_b.astype(BF16), wg_ref[blk]) + bg_ref[blk]
        g_rnn = proj_tm(OFF_GRNN + lo, GATE_BLOCK)
        for off in gate_offsets[blk * per_block:(blk + 1) * per_block]:
            gate_pieces.append(proj(off, piece))
        r = jax.nn.sigmoid(z[:, :GATE_BLOCK])
        i = jax.nn.sigmoid(z[:, GATE_BLOCK:])
        neg_log_a = r * decay_rate[:, lo:lo + GATE_BLOCK]
        a = jnp.exp(-neg_log_a)
        norm = _sqrt_nonneg(jnp.tanh(neg_log_a) * (1.0 + a * a))
        u = norm * (i * xr_b)

        hl = u[0:SUBLANES, :]
        pr = a[0:SUBLANES, :]
        local_states = [hl]
        decays = [pr]
        for s in range(1, steps):
            a_s = a[s * SUBLANES:(s + 1) * SUBLANES, :]
            hl = a_s * hl + u[s * SUBLANES:(s + 1) * SUBLANES, :]
            pr = pr * a_s
            local_states.append(hl)
            decays.append(pr)
        for k in (1, 2, 4):
            keep = row_b >= k
            p_prev = jnp.where(keep, pltpu.roll(pr, k, 0), 1.0)
            h_prev = jnp.where(keep, pltpu.roll(hl, k, 0), 0.0)
            hl = pr * h_prev + hl
            pr = pr * p_prev
        carry = hc_ref[:, lo:lo + GATE_BLOCK]
        ends = hl + pr * carry
        entry = jnp.where(row_b >= 1, _shift_chains(ends), carry)
        hc_ref[:, lo:lo + GATE_BLOCK] = jnp.broadcast_to(
            ends[SUBLANES - 1:SUBLANES, :], (SUBLANES, GATE_BLOCK))
        states = (jnp.concatenate(local_states, axis=0)
                  + jnp.concatenate(decays, axis=0) * jnp.concatenate([entry] * steps, axis=0))
        ya_blocks.append((states * _gelu(g_rnn)).astype(BF16))
    gate_a_raw = jnp.concatenate(gate_pieces[:D_MODEL // piece], axis=1)
    gate_b_raw = jnp.concatenate(gate_pieces[D_MODEL // piece:], axis=1)

    t_chunk = lax.broadcasted_iota(jnp.int32, (SGU_BLOCK, SGU_BLOCK), 0) // CHUNK
    s_chunk = lax.broadcasted_iota(jnp.int32, (SGU_BLOCK, SGU_BLOCK), 1) // CHUNK
    causal = t_chunk >= s_chunk
    ws = [jnp.where(causal, ws_ref[g], 0.0).astype(BF16) for g in range(SGU_GROUPS)]
    blocks = []
    for n in range(tile // SGU_BLOCK):
        rows = slice(n * SGU_BLOCK, (n + 1) * SGU_BLOCK)
        cols = [
            _dot(ws[g], vln[rows, g * SGU_GROUP_DIM:(g + 1) * SGU_GROUP_DIM])
            for g in range(SGU_GROUPS)
        ]
        blocks.append(jnp.concatenate(cols, axis=1) + bs_ref[...])
    mixed = jnp.concatenate(blocks, axis=0)

    ya_tok = _dot(unperm_ref[...], jnp.concatenate(ya_blocks, axis=1)).astype(BF16)
    ya = _dot(ya_tok, wa_ref[...])
    yb = _dot((_gelu(u_raw) * mixed).astype(BF16), wb_ref[...])

    merged = (jax.nn.sigmoid(gate_a_raw) * ya + jax.nn.sigmoid(gate_b_raw) * yb).astype(BF16)
    o_ref[...] = x + _dot(merged, wo_ref[...])


def _ffn_kernel(x_ref, ng_ref, wu_ref, wd_ref, fg_ref, o_ref, h_ref, *, final_norm):
    x = x_ref[...]
    h_ref[...] = _rmsnorm(x, ng_ref[...]).astype(BF16)
    acc = x
    for c in range(D_FF // FF_CHUNK):
        lo = c * FF_CHUNK
        f = jnp.maximum(_dot(h_ref[...], wu_ref[:, lo:lo + FF_CHUNK]), 0.0)
        acc = acc + _dot((f * f).astype(BF16), wd_ref[lo:lo + FF_CHUNK, :])
    if final_norm:
        acc = _rmsnorm(acc, fg_ref[...])
    o_ref[...] = acc


class _Layer:
    def __init__(self, stacked, layer):
        self.stacked = stacked
        self.layer = layer


def _resident_spec(p):
    if isinstance(p, _Layer):
        shape = p.stacked.shape[1:]
        layer = p.layer
        return pl.BlockSpec((None,) + shape, lambda i: (layer,) + (0,) * len(shape),
                            pipeline_mode=pl.Buffered(1))
    ndim = p.ndim
    return pl.BlockSpec(p.shape, lambda i: (0,) * ndim, pipeline_mode=pl.Buffered(1))


def _operand(p):
    return p.stacked if isinstance(p, _Layer) else p


def _time_major_permutation(tile):
    steps = tile // SUBLANES
    dst = np.arange(tile)
    src = (dst % SUBLANES) * steps + dst // SUBLANES
    perm = np.zeros((tile, tile), np.float32)
    perm[dst, src] = 1.0
    return perm


def _mixer_call(x2, params, *, seq_len):
    n_tok = x2.shape[0]
    tile = MIXER_TILE
    perm = _time_major_permutation(tile)
    params = (jnp.asarray(perm, BF16), jnp.asarray(perm.T, BF16)) + tuple(params)
    kern = functools.partial(_mixer_kernel, tile=tile, tiles_per_seq=seq_len // tile)
    tok_spec = pl.BlockSpec((tile, D_MODEL), lambda i: (i, 0))
    return pl.pallas_call(
        kern,
        grid=(n_tok // tile,),
        in_specs=[tok_spec] + [_resident_spec(p) for p in params],
        out_specs=tok_spec,
        out_shape=jax.ShapeDtypeStruct((n_tok, D_MODEL), F32),
        scratch_shapes=[
            pltpu.VMEM((tile, D_MODEL), BF16),
            pltpu.VMEM((tile, D_MODEL), BF16),
            pltpu.VMEM((CONV_TAIL, D_RNN), F32),
            pltpu.VMEM((SUBLANES, D_RNN), F32),
        ],
        compiler_params=pltpu.CompilerParams(
            dimension_semantics=("arbitrary",), vmem_limit_bytes=VMEM_LIMIT_BYTES),
        name="mixer",
    )(x2, *[_operand(p) for p in params])


def _ffn_call(x2, params, *, final_norm):
    n_tok = x2.shape[0]
    tile = FFN_TILE
    kern = functools.partial(_ffn_kernel, final_norm=final_norm)
    tok_spec = pl.BlockSpec((tile, D_MODEL), lambda i: (i, 0))
    return pl.pallas_call(
        kern,
        grid=(n_tok // tile,),
        in_specs=[tok_spec] + [_resident_spec(p) for p in params],
        out_specs=tok_spec,
        out_shape=jax.ShapeDtypeStruct((n_tok, D_MODEL), F32),
        scratch_shapes=[pltpu.VMEM((tile, D_MODEL), BF16)],
        compiler_params=pltpu.CompilerParams(
            dimension_semantics=("arbitrary",), vmem_limit_bytes=VMEM_LIMIT_BYTES),
        name="ffn",
    )(x2, *[_operand(p) for p in params])


def _gate_weights(w_a, w_x, b_a, b_x):
    eye = jnp.eye(HEADS_PER_GATE_BLOCK, dtype=w_a.dtype)

    def block_diag(w):
        w = w.reshape(N_GATE_BLOCKS, HEADS_PER_GATE_BLOCK, RNN_HEAD_DIM, RNN_HEAD_DIM)
        return jnp.einsum('jhik,hg->jhigk', w, eye).reshape(N_GATE_BLOCKS, GATE_BLOCK, GATE_BLOCK)

    wg = jnp.concatenate([block_diag(w_a), block_diag(w_x)], axis=-1).astype(BF16)
    bg = jnp.concatenate([b_a.reshape(N_GATE_BLOCKS, 1, GATE_BLOCK),
                          b_x.reshape(N_GATE_BLOCKS, 1, GATE_BLOCK)], axis=-1)
    return wg, bg


def kernel(x, norm_mix_g, w_in, conv_w, conv_b, lru_w_a, lru_b_a, lru_w_x, lru_b_x, lru_lambda, sgu_ln_g, sgu_ln_b, sgu_w_s, sgu_b_s, w_branch_a, w_branch_b, w_out, norm_ffn_g, w_up, w_down, final_norm_g):
    batch, seq_len, _ = x.shape
    depth = w_in.shape[0]
    assert seq_len % MIXER_TILE == 0 and (batch * seq_len) % FFN_TILE == 0
    x2 = x.reshape(batch * seq_len, D_MODEL)
    w_in, w_branch_a, w_branch_b, w_out, w_up, w_down = (
        w.astype(BF16) for w in (w_in, w_branch_a, w_branch_b, w_out, w_up, w_down))
    for l in range(depth):
        wg, bg = _gate_weights(lru_w_a[l], lru_w_x[l], lru_b_a[l], lru_b_x[l])
        bs = jnp.repeat(sgu_b_s[l].T, SGU_GROUP_DIM, axis=1)
        mixer_params = (
            norm_mix_g[l].reshape(1, D_MODEL),
            _Layer(w_in, l),
            conv_w[l],
            conv_b[l].reshape(1, D_RNN),
            wg, bg,
            lru_lambda[l].reshape(1, D_RNN),
            sgu_ln_g[l].reshape(1, D_SGU),
            sgu_ln_b[l].reshape(1, D_SGU),
            sgu_w_s[l],
            bs,
            _Layer(w_branch_a, l),
            _Layer(w_branch_b, l),
            _Layer(w_out, l),
        )
        x2 = _mixer_call(x2, mixer_params, seq_len=seq_len)
        ffn_params = (
            norm_ffn_g[l].reshape(1, D_MODEL),
            _Layer(w_up, l),
            _Layer(w_down, l),
            final_norm_g.reshape(1, D_MODEL),
        )
        x2 = _ffn_call(x2, ffn_params, final_norm=(l == depth - 1))
    return x2.reshape(batch, seq_len, D_MODEL)
```

```python
import functools

import jax
import jax.numpy as jnp
from jax import lax
from jax.experimental import pallas as pl
from jax.experimental.pallas import tpu as pltpu

D_MODEL = 1024
D_RNN = 1280
RNN_HEADS = 20
RNN_HEAD_DIM = D_RNN // RNN_HEADS
CONV_WIDTH = 4
LRU_C = 8.0
D_SGU = 1024
SGU_GROUPS = 8
SGU_GROUP_DIM = D_SGU // SGU_GROUPS
SGU_BLOCK = 128
CHUNK = 64
D_FF = 4 * D_MODEL
EPS = 1e-6

OFF_XRNN = 0
OFF_GRNN = OFF_XRNN + D_RNN
OFF_U = OFF_GRNN + D_RNN
OFF_V = OFF_U + D_SGU
OFF_GATE_A = OFF_V + D_SGU
OFF_GATE_B = OFF_GATE_A + D_MODEL
D_IN = OFF_GATE_B + D_MODEL

SUBLANES = 8
GATE_BLOCK = 256
N_GATE_BLOCKS = D_RNN // GATE_BLOCK
HEADS_PER_GATE_BLOCK = GATE_BLOCK // RNN_HEAD_DIM
CONV_TAIL = (CONV_WIDTH - 1) * SUBLANES
FF_CHUNK = 1024

MIXER_TILE = 256
FFN_TILE = 1024
VMEM_LIMIT_BYTES = 56 * 1024 * 1024

F32 = jnp.float32
BF16 = jnp.bfloat16


def _dot(a, b):
    return jnp.dot(a, b, preferred_element_type=F32)


def _rmsnorm(x, g):
    return x * lax.rsqrt(jnp.mean(x * x, axis=-1, keepdims=True) + EPS) * g


def _gelu(x):
    c = -2.0 * 0.7978845608028654 * 1.4426950408889634
    return x / (1.0 + jnp.exp2(x * (c + (c * 0.044715) * (x * x))))


def _sqrt_nonneg(y):
    return jnp.where(y > 0.0, y * lax.rsqrt(y), 0.0)


def _shift_chains(v):
    return pltpu.roll(v, 1, 0)


def _mixer_kernel(x_ref, ng_ref, win_ref, cw_ref, cb_ref, wg_ref, bg_ref,
                  lam_ref, lng_ref, lnb_ref, ws_ref, bs_ref, wa_ref, wb_ref, wo_ref,
                  o_ref, h_ref, hp_ref, tail_ref, hc_ref, *, tile, tiles_per_seq):
    steps = tile // SUBLANES
    seq_tile = pl.program_id(0) % tiles_per_seq

    @pl.when(seq_tile == 0)
    def _():
        tail_ref[...] = jnp.zeros((CONV_TAIL, D_RNN), F32)
        hc_ref[...] = jnp.zeros((SUBLANES, D_RNN), F32)

    x = x_ref[...]
    hn = _rmsnorm(x, ng_ref[...])
    h_ref[...] = hn.astype(BF16)
    hp_ref[...] = jnp.swapaxes(
        hn.reshape(SUBLANES, steps, D_MODEL), 0, 1).reshape(tile, D_MODEL).astype(BF16)

    def proj(lo, width):
        return _dot(h_ref[...], win_ref[:, lo:lo + width])

    def proj_tm(lo, width):
        return _dot(hp_ref[...], win_ref[:, lo:lo + width])

    x_rnn = proj_tm(OFF_XRNN, D_RNN)
    v_raw = proj(OFF_V, D_SGU)

    tail = x_rnn[tile - CONV_TAIL:, :]
    row = lax.broadcasted_iota(jnp.int32, (SUBLANES, D_RNN), 0)
    prev_tail = tail_ref[...]
    head = [
        _shift_chains(jnp.where(row == SUBLANES - 1,
                                prev_tail[q * SUBLANES:(q + 1) * SUBLANES, :],
                                tail[q * SUBLANES:(q + 1) * SUBLANES, :]))
        for q in range(CONV_WIDTH - 1)
    ]
    tail_ref[...] = tail
    x_ext = jnp.concatenate(head + [x_rnn], axis=0)
    xr = cb_ref[...]
    for k in range(CONV_WIDTH):
        xr = xr + x_ext[k * SUBLANES:k * SUBLANES + tile, :] * cw_ref[k:k + 1, :]

    gv = _gelu(v_raw)
    mu = jnp.mean(gv, axis=-1, keepdims=True)
    vc = gv - mu
    var = jnp.mean(vc * vc, axis=-1, keepdims=True)
    vln = (vc * lax.rsqrt(var + EPS) * lng_ref[...] + lnb_ref[...]).astype(BF16)
    u_raw = proj(OFF_U, D_SGU)

    piece = GATE_BLOCK
    gate_offsets = [OFF_GATE_A + c for c in range(0, 2 * D_MODEL, piece)]
    per_block = -(-len(gate_offsets) // N_GATE_BLOCKS)
    gate_pieces = []

    neg_lam = -lam_ref[...]
    softplus = jnp.maximum(neg_lam, 0.0) + jnp.log1p(jnp.exp(-jnp.abs(neg_lam)))
    decay_rate = LRU_C * softplus
    row_b = lax.broadcasted_iota(jnp.int32, (SUBLANES, GATE_BLOCK), 0)
    ya_blocks = []
    for blk in range(N_GATE_BLOCKS):
        lo = blk * GATE_BLOCK
        xr_b = xr[:, lo:lo + GATE_BLOCK]
        z = _dot(xr_b.astype(BF16), wg_ref[blk]) + bg_ref[blk]
        g_rnn = proj_tm(OFF_GRNN + lo, GATE_BLOCK)
        for off in gate_offsets[blk * per_block:(blk + 1) * per_block]:
            gate_pieces.append(proj(off, piece))
        r = jax.nn.sigmoid(z[:, :GATE_BLOCK])
        i = jax.nn.sigmoid(z[:, GATE_BLOCK:])
        neg_log_a = r * decay_rate[:, lo:lo + GATE_BLOCK]
        a = jnp.exp(-neg_log_a)
        norm = _sqrt_nonneg(jnp.tanh(neg_log_a) * (1.0 + a * a))
        u = norm * (i * xr_b)

        hl = u[0:SUBLANES, :]
        pr = a[0:SUBLANES, :]
        local_states = [hl]
        decays = [pr]
        for s in range(1, steps):
            a_s = a[s * SUBLANES:(s + 1) * SUBLANES, :]
            hl = a_s * hl + u[s * SUBLANES:(s + 1) * SUBLANES, :]
            pr = pr * a_s
            local_states.append(hl)
            decays.append(pr)
        for k in (1, 2, 4):
            keep = row_b >= k
            p_prev = jnp.where(keep, pltpu.roll(pr, k, 0), 1.0)
            h_prev = jnp.where(keep, pltpu.roll(hl, k, 0), 0.0)
            hl = pr * h_prev + hl
            pr = pr * p_prev
        carry = hc_ref[:, lo:lo + GATE_BLOCK]
        ends = hl + pr * carry
        entry = jnp.where(row_b >= 1, _shift_chains(ends), carry)
        hc_ref[:, lo:lo + GATE_BLOCK] = jnp.broadcast_to(
            ends[SUBLANES - 1:SUBLANES, :], (SUBLANES, GATE_BLOCK))
        states = (jnp.concatenate(local_states, axis=0)
                  + jnp.concatenate(decays, axis=0) * jnp.concatenate([entry] * steps, axis=0))
        ya_blocks.append(states * _gelu(g_rnn))
    gate_a_raw = jnp.concatenate(gate_pieces[:D_MODEL // piece], axis=1)
    gate_b_raw = jnp.concatenate(gate_pieces[D_MODEL // piece:], axis=1)

    t_chunk = lax.broadcasted_iota(jnp.int32, (SGU_BLOCK, SGU_BLOCK), 0) // CHUNK
    s_chunk = lax.broadcasted_iota(jnp.int32, (SGU_BLOCK, SGU_BLOCK), 1) // CHUNK
    causal = t_chunk >= s_chunk
    ws = [jnp.where(causal, ws_ref[g], 0.0).astype(BF16) for g in range(SGU_GROUPS)]
    blocks = []
    for n in range(tile // SGU_BLOCK):
        rows = slice(n * SGU_BLOCK, (n + 1) * SGU_BLOCK)
        cols = [
            _dot(ws[g], vln[rows, g * SGU_GROUP_DIM:(g + 1) * SGU_GROUP_DIM])
            for g in range(SGU_GROUPS)
        ]
        blocks.append(jnp.concatenate(cols, axis=1) + bs_ref[...])
    mixed = jnp.concatenate(blocks, axis=0)

    ya_tok = jnp.concatenate(
        [jnp.swapaxes(blk.reshape(steps, SUBLANES, GATE_BLOCK), 0, 1)
         .reshape(tile, GATE_BLOCK).astype(BF16) for blk in ya_blocks], axis=1)
    ya = _dot(ya_tok, wa_ref[...])
    yb = _dot((_gelu(u_raw) * mixed).astype(BF16), wb_ref[...])

    merged = (jax.nn.sigmoid(gate_a_raw) * ya + jax.nn.sigmoid(gate_b_raw) * yb).astype(BF16)
    o_ref[...] = x + _dot(merged, wo_ref[...])


def _ffn_kernel(x_ref, ng_ref, wu_ref, wd_ref, fg_ref, o_ref, h_ref, *, final_norm):
    x = x_ref[...]
    h_ref[...] = _rmsnorm(x, ng_ref[...]).astype(BF16)
    acc = x
    for c in range(D_FF // FF_CHUNK):
        lo = c * FF_CHUNK
        f = jnp.maximum(_dot(h_ref[...], wu_ref[:, lo:lo + FF_CHUNK]), 0.0)
        acc = acc + _dot((f * f).astype(BF16), wd_ref[lo:lo + FF_CHUNK, :])
    if final_norm:
        acc = _rmsnorm(acc, fg_ref[...])
    o_ref[...] = acc


class _Layer:
    def __init__(self, stacked, layer):
        self.stacked = stacked
        self.layer = layer


def _resident_spec(p):
    if isinstance(p, _Layer):
        shape = p.stacked.shape[1:]
        layer = p.layer
        return pl.BlockSpec((None,) + shape, lambda i: (layer,) + (0,) * len(shape),
                            pipeline_mode=pl.Buffered(1))
    ndim = p.ndim
    return pl.BlockSpec(p.shape, lambda i: (0,) * ndim, pipeline_mode=pl.Buffered(1))


def _operand(p):
    return p.stacked if isinstance(p, _Layer) else p


def _mixer_call(x2, params, *, seq_len):
    n_tok = x2.shape[0]
    tile = MIXER_TILE
    kern = functools.partial(_mixer_kernel, tile=tile, tiles_per_seq=seq_len // tile)
    tok_spec = pl.BlockSpec((tile, D_MODEL), lambda i: (i, 0))
    return pl.pallas_call(
        kern,
        grid=(n_tok // tile,),
        in_specs=[tok_spec] + [_resident_spec(p) for p in params],
        out_specs=tok_spec,
        out_shape=jax.ShapeDtypeStruct((n_tok, D_MODEL), F32),
        scratch_shapes=[
            pltpu.VMEM((tile, D_MODEL), BF16),
            pltpu.VMEM((tile, D_MODEL), BF16),
            pltpu.VMEM((CONV_TAIL, D_RNN), F32),
            pltpu.VMEM((SUBLANES, D_RNN), F32),
        ],
        compiler_params=pltpu.CompilerParams(
            dimension_semantics=("arbitrary",), vmem_limit_bytes=VMEM_LIMIT_BYTES),
        name="mixer",
    )(x2, *[_operand(p) for p in params])


def _ffn_call(x2, params, *, final_norm):
    n_tok = x2.shape[0]
    tile = FFN_TILE
    kern = functools.partial(_ffn_kernel, final_norm=final_norm)
    tok_spec = pl.BlockSpec((tile, D_MODEL), lambda i: (i, 0))
    return pl.pallas_call(
        kern,
        grid=(n_tok // tile,),
        in_specs=[tok_spec] + [_resident_spec(p) for p in params],
        out_specs=tok_spec,
        out_shape=jax.ShapeDtypeStruct((n_tok, D_MODEL), F32),
        scratch_shapes=[pltpu.VMEM((tile, D_MODEL), BF16)],
        compiler_params=pltpu.CompilerParams(
            dimension_semantics=("arbitrary",), vmem_limit_bytes=VMEM_LIMIT_BYTES),
        name="ffn",
    )(x2, *[_operand(p) for p in params])


def _gate_weights(w_a, w_x, b_a, b_x):
    eye = jnp.eye(HEADS_PER_GATE_BLOCK, dtype=w_a.dtype)

    def block_diag(w):
        w = w.reshape(N_GATE_BLOCKS, HEADS_PER_GATE_BLOCK, RNN_HEAD_DIM, RNN_HEAD_DIM)
        return jnp.einsum('jhik,hg->jhigk', w, eye).reshape(N_GATE_BLOCKS, GATE_BLOCK, GATE_BLOCK)

    wg = jnp.concatenate([block_diag(w_a), block_diag(w_x)], axis=-1).astype(BF16)
    bg = jnp.concatenate([b_a.reshape(N_GATE_BLOCKS, 1, GATE_BLOCK),
                          b_x.reshape(N_GATE_BLOCKS, 1, GATE_BLOCK)], axis=-1)
    return wg, bg


def kernel(x, norm_mix_g, w_in, conv_w, conv_b, lru_w_a, lru_b_a, lru_w_x, lru_b_x, lru_lambda, sgu_ln_g, sgu_ln_b, sgu_w_s, sgu_b_s, w_branch_a, w_branch_b, w_out, norm_ffn_g, w_up, w_down, final_norm_g):
    batch, seq_len, _ = x.shape
    depth = w_in.shape[0]
    assert seq_len % MIXER_TILE == 0 and (batch * seq_len) % FFN_TILE == 0
    x2 = x.reshape(batch * seq_len, D_MODEL)
    w_in, w_branch_a, w_branch_b, w_out, w_up, w_down = (
        w.astype(BF16) for w in (w_in, w_branch_a, w_branch_b, w_out, w_up, w_down))
    for l in range(depth):
        wg, bg = _gate_weights(lru_w_a[l], lru_w_x[l], lru_b_a[l], lru_b_x[l])
        bs = jnp.repeat(sgu_b_s[l].T, SGU_GROUP_DIM, axis=1)
        mixer_params = (
            norm_mix_g[l].reshape(1, D_MODEL),
            _Layer(w_in, l),
            conv_w[l],
            conv_b[l].reshape(1, D_RNN),
            wg, bg,
            lru_lambda[l].reshape(1, D_RNN),
            sgu_ln_g[l].reshape(1, D_SGU),
            sgu_ln_b[l].reshape(1, D_SGU),
            sgu_w_s[l],
            bs,
            _Layer(w_branch_a, l),
            _Layer(w_branch_b, l),
            _Layer(w_out, l),
        )
        x2 = _mixer_call(x2, mixer_params, seq_len=seq_len)
        ffn_params = (
            norm_ffn_g[l].reshape(1, D_MODEL),
            _Layer(w_up, l),
            _Layer(w_down, l),
            final_norm_g.reshape(1, D_MODEL),
        )
        x2 = _ffn_call(x2, ffn_params, final_norm=(l == depth - 1))
    return x2.reshape(batch, seq_len, D_MODEL)
```

```python
import functools

import jax
import jax.numpy as jnp
from jax import lax
from jax.experimental import pallas as pl
from jax.experimental.pallas import tpu as pltpu

D_MODEL = 1024
D_RNN = 1280
RNN_HEADS = 20
RNN_HEAD_DIM = D_RNN // RNN_HEADS
CONV_WIDTH = 4
LRU_C = 8.0
D_SGU = 1024
SGU_GROUPS = 8
SGU_GROUP_DIM = D_SGU // SGU_GROUPS
SGU_BLOCK = 128
CHUNK = 64
D_FF = 4 * D_MODEL
EPS = 1e-6

OFF_XRNN = 0
OFF_GRNN = OFF_XRNN + D_RNN
OFF_U = OFF_GRNN + D_RNN
OFF_V = OFF_U + D_SGU
OFF_GATE_A = OFF_V + D_SGU
OFF_GATE_B = OFF_GATE_A + D_MODEL
D_IN = OFF_GATE_B + D_MODEL

SUBLANES = 8
GATE_BLOCK = 256
N_GATE_BLOCKS = D_RNN // GATE_BLOCK
HEADS_PER_GATE_BLOCK = GATE_BLOCK // RNN_HEAD_DIM
CONV_TAIL = (CONV_WIDTH - 1) * SUBLANES
FF_CHUNK = 1024

MIXER_TILE = 512
FFN_TILE = 1024
VMEM_LIMIT_BYTES = 56 * 1024 * 1024

F32 = jnp.float32
BF16 = jnp.bfloat16


def _dot(a, b):
    return jnp.dot(a, b, preferred_element_type=F32)


def _rmsnorm(x, g):
    return x * lax.rsqrt(jnp.mean(x * x, axis=-1, keepdims=True) + EPS) * g


def _gelu(x):
    c = -2.0 * 0.7978845608028654 * 1.4426950408889634
    return x / (1.0 + jnp.exp2(x * (c + (c * 0.044715) * (x * x))))


def _sqrt_nonneg(y):
    return jnp.where(y > 0.0, y * lax.rsqrt(y), 0.0)


def _shift_chains(v):
    return pltpu.roll(v, 1, 0)


def _mixer_kernel(x_ref, ng_ref, win_ref, cw_ref, cb_ref, wg_ref, bg_ref,
                  lam_ref, lng_ref, lnb_ref, ws_ref, bs_ref, wa_ref, wb_ref, wo_ref,
                  o_ref, h_ref, hp_ref, tail_ref, hc_ref, *, tile, tiles_per_seq):
    steps = tile // SUBLANES
    seq_tile = pl.program_id(0) % tiles_per_seq

    @pl.when(seq_tile == 0)
    def _():
        tail_ref[...] = jnp.zeros((CONV_TAIL, D_RNN), F32)
        hc_ref[...] = jnp.zeros((SUBLANES, D_RNN), F32)

    x = x_ref[...]
    hn = _rmsnorm(x, ng_ref[...])
    h_ref[...] = hn.astype(BF16)
    hp_ref[...] = jnp.swapaxes(
        hn.reshape(SUBLANES, steps, D_MODEL), 0, 1).reshape(tile, D_MODEL).astype(BF16)

    def proj(lo, width):
        return _dot(h_ref[...], win_ref[:, lo:lo + width])

    def proj_tm(lo, width):
        return _dot(hp_ref[...], win_ref[:, lo:lo + width])

    x_rnn = proj_tm(OFF_XRNN, D_RNN)
    v_raw = proj(OFF_V, D_SGU)

    tail = x_rnn[tile - CONV_TAIL:, :]
    row = lax.broadcasted_iota(jnp.int32, (SUBLANES, D_RNN), 0)
    prev_tail = tail_ref[...]
    head = [
        _shift_chains(jnp.where(row == SUBLANES - 1,
                                prev_tail[q * SUBLANES:(q + 1) * SUBLANES, :],
                                tail[q * SUBLANES:(q + 1) * SUBLANES, :]))
        for q in range(CONV_WIDTH - 1)
    ]
    tail_ref[...] = tail
    x_ext = jnp.concatenate(head + [x_rnn], axis=0)
    xr = cb_ref[...]
    for k in range(CONV_WIDTH):
        xr = xr + x_ext[k * SUBLANES:k * SUBLANES + tile, :] * cw_ref[k:k + 1, :]

    gv = _gelu(v_raw)
    mu = jnp.mean(gv, axis=-1, keepdims=True)
    vc = gv - mu
    var = jnp.mean(vc * vc, axis=-1, keepdims=True)
    vln = (vc * lax.rsqrt(var + EPS) * lng_ref[...] + lnb_ref[...]).astype(BF16)
    u_raw = proj(OFF_U, D_SGU)

    piece = GATE_BLOCK
    gate_offsets = [OFF_GATE_A + c for c in range(0, 2 * D_MODEL, piece)]
    per_block = -(-len(gate_offsets) // N_GATE_BLOCKS)
    gate_pieces = []

    neg_lam = -lam_ref[...]
    softplus = jnp.maximum(neg_lam, 0.0) + jnp.log1p(jnp.exp(-jnp.abs(neg_lam)))
    decay_rate = LRU_C * softplus
    row_b = lax.broadcasted_iota(jnp.int32, (SUBLANES, GATE_BLOCK), 0)
    ya_blocks = []
    for blk in range(N_GATE_BLOCKS):
        lo = blk * GATE_BLOCK
        xr_b = xr[:, lo:lo + GATE_BLOCK]
        z = _dot(xr_b.astype(BF16), wg_ref[blk]) + bg_ref[blk]
        g_rnn = proj_tm(OFF_GRNN + lo, GATE_BLOCK)
        for off in gate_offsets[blk * per_block:(blk + 1) * per_block]:
            gate_pieces.append(proj(off, piece))
        r = jax.nn.sigmoid(z[:, :GATE_BLOCK])
        i = jax.nn.sigmoid(z[:, GATE_BLOCK:])
        neg_log_a = r * decay_rate[:, lo:lo + GATE_BLOCK]
        a = jnp.exp(-neg_log_a)
        norm = _sqrt_nonneg(jnp.tanh(neg_log_a) * (1.0 + a * a))
        u = norm * (i * xr_b)

        hl = u[0:SUBLANES, :]
        pr = a[0:SUBLANES, :]
        local_states = [hl]
        decays = [pr]
        for s in range(1, steps):
            a_s = a[s * SUBLANES:(s + 1) * SUBLANES, :]
            hl = a_s * hl + u[s * SUBLANES:(s + 1) * SUBLANES, :]
            pr = pr * a_s
            local_states.append(hl)
            decays.append(pr)
        for k in (1, 2, 4):
            keep = row_b >= k
            p_prev = jnp.where(keep, pltpu.roll(pr, k, 0), 1.0)
            h_prev = jnp.where(keep, pltpu.roll(hl, k, 0), 0.0)
            hl = pr * h_prev + hl
            pr = pr * p_prev
        carry = hc_ref[:, lo:lo + GATE_BLOCK]
        ends = hl + pr * carry
        entry = jnp.where(row_b >= 1, _shift_chains(ends), carry)
        hc_ref[:, lo:lo + GATE_BLOCK] = jnp.broadcast_to(
            ends[SUBLANES - 1:SUBLANES, :], (SUBLANES, GATE_BLOCK))
        states = (jnp.concatenate(local_states, axis=0)
                  + jnp.concatenate(decays, axis=0) * jnp.concatenate([entry] * steps, axis=0))
        ya_blocks.append(states * _gelu(g_rnn))
    gate_a_raw = jnp.concatenate(gate_pieces[:D_MODEL // piece], axis=1)
    gate_b_raw = jnp.concatenate(gate_pieces[D_MODEL // piece:], axis=1)

    t_chunk = lax.broadcasted_iota(jnp.int32, (SGU_BLOCK, SGU_BLOCK), 0) // CHUNK
    s_chunk = lax.broadcasted_iota(jnp.int32, (SGU_BLOCK, SGU_BLOCK), 1) // CHUNK
    causal = t_chunk >= s_chunk
    ws = [jnp.where(causal, ws_ref[g], 0.0).astype(BF16) for g in range(SGU_GROUPS)]
    blocks = []
    for n in range(tile // SGU_BLOCK):
        rows = slice(n * SGU_BLOCK, (n + 1) * SGU_BLOCK)
        cols = [
            _dot(ws[g], vln[rows, g * SGU_GROUP_DIM:(g + 1) * SGU_GROUP_DIM])
            for g in range(SGU_GROUPS)
        ]
        blocks.append(jnp.concatenate(cols, axis=1) + bs_ref[...])
    mixed = jnp.concatenate(blocks, axis=0)

    ya_tok = jnp.concatenate(
        [jnp.swapaxes(blk.reshape(steps, SUBLANES, GATE_BLOCK), 0, 1)
         .reshape(tile, GATE_BLOCK).astype(BF16) for blk in ya_blocks], axis=1)
    ya = _dot(ya_tok, wa_ref[...])
    yb = _dot((_gelu(u_raw) * mixed).astype(BF16), wb_ref[...])

    merged = (jax.nn.sigmoid(gate_a_raw) * ya + jax.nn.sigmoid(gate_b_raw) * yb).astype(BF16)
    o_ref[...] = x + _dot(merged, wo_ref[...])


def _ffn_kernel(x_ref, ng_ref, wu_ref, wd_ref, fg_ref, o_ref, h_ref, *, final_norm):
    x = x_ref[...]
    h_ref[...] = _rmsnorm(x, ng_ref[...]).astype(BF16)
    acc = x
    for c in range(D_FF // FF_CHUNK):
        lo = c * FF_CHUNK
        f = jnp.maximum(_dot(h_ref[...], wu_ref[:, lo:lo + FF_CHUNK]), 0.0)
        acc = acc + _dot((f * f).astype(BF16), wd_ref[lo:lo + FF_CHUNK, :])
    if final_norm:
        acc = _rmsnorm(acc, fg_ref[...])
    o_ref[...] = acc


class _Layer:
    def __init__(self, stacked, layer):
        self.stacked = stacked
        self.layer = layer


def _resident_spec(p):
    if isinstance(p, _Layer):
        shape = p.stacked.shape[1:]
        layer = p.layer
        return pl.BlockSpec((None,) + shape, lambda i: (layer,) + (0,) * len(shape),
                            pipeline_mode=pl.Buffered(1))
    ndim = p.ndim
    return pl.BlockSpec(p.shape, lambda i: (0,) * ndim, pipeline_mode=pl.Buffered(1))


def _operand(p):
    return p.stacked if isinstance(p, _Layer) else p


def _mixer_call(x2, params, *, seq_len):
    n_tok = x2.shape[0]
    tile = MIXER_TILE
    kern = functools.partial(_mixer_kernel, tile=tile, tiles_per_seq=seq_len // tile)
    tok_spec = pl.BlockSpec((tile, D_MODEL), lambda i: (i, 0))
    return pl.pallas_call(
        kern,
        grid=(n_tok // tile,),
        in_specs=[tok_spec] + [_resident_spec(p) for p in params],
        out_specs=tok_spec,
        out_shape=jax.ShapeDtypeStruct((n_tok, D_MODEL), F32),
        scratch_shapes=[
            pltpu.VMEM((tile, D_MODEL), BF16),
            pltpu.VMEM((tile, D_MODEL), BF16),
            pltpu.VMEM((CONV_TAIL, D_RNN), F32),
            pltpu.VMEM((SUBLANES, D_RNN), F32),
        ],
        compiler_params=pltpu.CompilerParams(
            dimension_semantics=("arbitrary",), vmem_limit_bytes=VMEM_LIMIT_BYTES),
        name="mixer",
    )(x2, *[_operand(p) for p in params])


def _ffn_call(x2, params, *, final_norm):
    n_tok = x2.shape[0]
    tile = FFN_TILE
    kern = functools.partial(_ffn_kernel, final_norm=final_norm)
    tok_spec = pl.BlockSpec((tile, D_MODEL), lambda i: (i, 0))
    return pl.pallas_call(
        kern,
        grid=(n_tok // tile,),
        in_specs=[tok_spec] + [_resident_spec(p) for p in params],
        out_specs=tok_spec,
        out_shape=jax.ShapeDtypeStruct((n_tok, D_MODEL), F32),
        scratch_shapes=[pltpu.VMEM((tile, D_MODEL), BF16)],
        compiler_params=pltpu.CompilerParams(
            dimension_semantics=("arbitrary",), vmem_limit_bytes=VMEM_LIMIT_BYTES),
        name="ffn",
    )(x2, *[_operand(p) for p in params])


def _gate_weights(w_a, w_x, b_a, b_x):
    eye = jnp.eye(HEADS_PER_GATE_BLOCK, dtype=w_a.dtype)

    def block_diag(w):
        w = w.reshape(N_GATE_BLOCKS, HEADS_PER_GATE_BLOCK, RNN_HEAD_DIM, RNN_HEAD_DIM)
        return jnp.einsum('jhik,hg->jhigk', w, eye).reshape(N_GATE_BLOCKS, GATE_BLOCK, GATE_BLOCK)

    wg = jnp.concatenate([block_diag(w_a), block_diag(w_x)], axis=-1).astype(BF16)
    bg = jnp.concatenate([b_a.reshape(N_GATE_BLOCKS, 1, GATE_BLOCK),
                          b_x.reshape(N_GATE_BLOCKS, 1, GATE_BLOCK)], axis=-1)
    return wg, bg


def kernel(x, norm_mix_g, w_in, conv_w, conv_b, lru_w_a, lru_b_a, lru_w_x, lru_b_x, lru_lambda, sgu_ln_g, sgu_ln_b, sgu_w_s, sgu_b_s, w_branch_a, w_branch_b, w_out, norm_ffn_g, w_up, w_down, final_norm_g):
    batch, seq_len, _ = x.shape
    depth = w_in.shape[0]
    assert seq_len % MIXER_TILE == 0 and (batch * seq_len) % FFN_TILE == 0
    x2 = x.reshape(batch * seq_len, D_MODEL)
    w_in, w_branch_a, w_branch_b, w_out, w_up, w_down = (
        w.astype(BF16) for w in (w_in, w_branch_a, w_branch_b, w_out, w_up, w_down))
    for l in range(depth):
        wg, bg = _gate_weights(lru_w_a[l], lru_w_x[l], lru_b_a[l], lru_b_x[l])
        bs = jnp.repeat(sgu_b_s[l].T, SGU_GROUP_DIM, axis=1)
        mixer_params = (
            norm_mix_g[l].reshape(1, D_MODEL),
            _Layer(w_in, l),
            conv_w[l],
            conv_b[l].reshape(1, D_RNN),
            wg, bg,
            lru_lambda[l].reshape(1, D_RNN),
            sgu_ln_g[l].reshape(1, D_SGU),
            sgu_ln_b[l].reshape(1, D_SGU),
            sgu_w_s[l],
            bs,
            _Layer(w_branch_a, l),
            _Layer(w_branch_b, l),
            _Layer(w_out, l),
        )
        x2 = _mixer_call(x2, mixer_params, seq_len=seq_len)
        ffn_params = (
            norm_ffn_g[l].reshape(1, D_MODEL),
            _Layer(w_up, l),
            _Layer(w_down, l),
            final_norm_g.reshape(1, D_MODEL),
        )
        x2 = _ffn_call(x2, ffn_params, final_norm=(l == depth - 1))
    return x2.reshape(batch, seq_len, D_MODEL)
```

```python
import functools

import jax
import jax.numpy as jnp
from jax import lax
from jax.experimental import pallas as pl
from jax.experimental.pallas import tpu as pltpu

D_MODEL = 1024
D_RNN = 1280
RNN_HEADS = 20
RNN_HEAD_DIM = D_RNN // RNN_HEADS
CONV_WIDTH = 4
LRU_C = 8.0
D_SGU = 1024
SGU_GROUPS = 8
SGU_GROUP_DIM = D_SGU // SGU_GROUPS
SGU_BLOCK = 128
CHUNK = 64
D_FF = 4 * D_MODEL
EPS = 1e-6

OFF_XRNN = 0
OFF_GRNN = OFF_XRNN + D_RNN
OFF_U = OFF_GRNN + D_RNN
OFF_V = OFF_U + D_SGU
OFF_GATE_A = OFF_V + D_SGU
OFF_GATE_B = OFF_GATE_A + D_MODEL
D_IN = OFF_GATE_B + D_MODEL

SUBLANES = 8
GATE_BLOCK = 256
N_GATE_BLOCKS = D_RNN // GATE_BLOCK
HEADS_PER_GATE_BLOCK = GATE_BLOCK // RNN_HEAD_DIM
CONV_TAIL = (CONV_WIDTH - 1) * SUBLANES
FF_CHUNK = 1024

MIXER_TILE = 256
FFN_TILE = 1024
VMEM_LIMIT_BYTES = 56 * 1024 * 1024

F32 = jnp.float32
BF16 = jnp.bfloat16


def _dot(a, b):
    return jnp.dot(a, b, preferred_element_type=F32)


def _rmsnorm(x, g):
    return x * lax.rsqrt(jnp.mean(x * x, axis=-1, keepdims=True) + EPS) * g


def _gelu(x):
    c = -2.0 * 0.7978845608028654 * 1.4426950408889634
    return x / (1.0 + jnp.exp2(x * (c + (c * 0.044715) * (x * x))))


def _sqrt_nonneg(y):
    return jnp.where(y > 0.0, y * lax.rsqrt(y), 0.0)


def _shift_chains(v):
    return pltpu.roll(v, 1, 0)


def _mixer_kernel(x_ref, ng_ref, win_ref, cw_ref, cb_ref, wg_ref, bg_ref,
                  lam_ref, lng_ref, lnb_ref, ws_ref, bs_ref, wa_ref, wb_ref, wo_ref,
                  o_ref, h_ref, hp_ref, tail_ref, hc_ref, *, tile, tiles_per_seq):
    steps = tile // SUBLANES
    seq_tile = pl.program_id(0) % tiles_per_seq

    @pl.when(seq_tile == 0)
    def _():
        tail_ref[...] = jnp.zeros((CONV_TAIL, D_RNN), F32)
        hc_ref[...] = jnp.zeros((SUBLANES, D_RNN), F32)

    x = x_ref[...]
    hn = _rmsnorm(x, ng_ref[...])
    h_ref[...] = hn.astype(BF16)
    hp_ref[...] = jnp.swapaxes(
        hn.reshape(SUBLANES, steps, D_MODEL), 0, 1).reshape(tile, D_MODEL).astype(BF16)

    def proj(lo, width):
        return _dot(h_ref[...], win_ref[:, lo:lo + width])

    def proj_tm(lo, width):
        return _dot(hp_ref[...], win_ref[:, lo:lo + width])

    v_raw = proj(OFF_V, D_SGU)
    x_rnn = proj_tm(OFF_XRNN, D_RNN)

    tail = x_rnn[tile - CONV_TAIL:, :]
    row = lax.broadcasted_iota(jnp.int32, (SUBLANES, D_RNN), 0)
    prev_tail = tail_ref[...]
    head = [
        _shift_chains(jnp.where(row == SUBLANES - 1,
                                prev_tail[q * SUBLANES:(q + 1) * SUBLANES, :],
                                tail[q * SUBLANES:(q + 1) * SUBLANES, :]))
        for q in range(CONV_WIDTH - 1)
    ]
    tail_ref[...] = tail
    x_ext = jnp.concatenate(head + [x_rnn], axis=0)
    xr = cb_ref[...]
    for k in range(CONV_WIDTH):
        xr = xr + x_ext[k * SUBLANES:k * SUBLANES + tile, :] * cw_ref[k:k + 1, :]

    gv = _gelu(v_raw)
    mu = jnp.mean(gv, axis=-1, keepdims=True)
    vc = gv - mu
    var = jnp.mean(vc * vc, axis=-1, keepdims=True)
    vln = (vc * lax.rsqrt(var + EPS) * lng_ref[...] + lnb_ref[...]).astype(BF16)
    u_raw = proj(OFF_U, D_SGU)

    piece = GATE_BLOCK
    gate_offsets = [OFF_GATE_A + c for c in range(0, 2 * D_MODEL, piece)]
    per_block = -(-len(gate_offsets) // N_GATE_BLOCKS)
    gate_pieces = []

    neg_lam = -lam_ref[...]
    softplus = jnp.maximum(neg_lam, 0.0) + jnp.log1p(jnp.exp(-jnp.abs(neg_lam)))
    decay_rate = LRU_C * softplus
    row_b = lax.broadcasted_iota(jnp.int32, (SUBLANES, GATE_BLOCK), 0)
    ya_blocks = []
    for blk in range(N_GATE_BLOCKS):
        lo = blk * GATE_BLOCK
        xr_b = xr[:, lo:lo + GATE_BLOCK]
        z = _dot(xr_b.astype(BF16), wg_ref[blk]) + bg_ref[blk]
        g_rnn = proj_tm(OFF_GRNN + lo, GATE_BLOCK)
        for off in gate_offsets[blk * per_block:(blk + 1) * per_block]:
            gate_pieces.append(proj(off, piece))
        r = jax.nn.sigmoid(z[:, :GATE_BLOCK])
        i = jax.nn.sigmoid(z[:, GATE_BLOCK:])
        neg_log_a = r * decay_rate[:, lo:lo + GATE_BLOCK]
        a = jnp.exp(-neg_log_a)
        norm = _sqrt_nonneg(jnp.tanh(neg_log_a) * (1.0 + a * a))
        u = norm * (i * xr_b)

        hl = u[0:SUBLANES, :]
        pr = a[0:SUBLANES, :]
        local_states = [hl]
        decays = [pr]
        for s in range(1, steps):
            a_s = a[s * SUBLANES:(s + 1) * SUBLANES, :]
            hl = a_s * hl + u[s * SUBLANES:(s + 1) * SUBLANES, :]
            pr = pr * a_s
            local_states.append(hl)
            decays.append(pr)
        for k in (1, 2, 4):
            keep = row_b >= k
            p_prev = jnp.where(keep, pltpu.roll(pr, k, 0), 1.0)
            h_prev = jnp.where(keep, pltpu.roll(hl, k, 0), 0.0)
            hl = pr * h_prev + hl
            pr = pr * p_prev
        carry = hc_ref[:, lo:lo + GATE_BLOCK]
        ends = hl + pr * carry
        entry = jnp.where(row_b >= 1, _shift_chains(ends), carry)
        hc_ref[:, lo:lo + GATE_BLOCK] = jnp.broadcast_to(
            ends[SUBLANES - 1:SUBLANES, :], (SUBLANES, GATE_BLOCK))
        states = (jnp.concatenate(local_states, axis=0)
                  + jnp.concatenate(decays, axis=0) * jnp.concatenate([entry] * steps, axis=0))
        ya_blocks.append(states * _gelu(g_rnn))
    gate_a_raw = jnp.concatenate(gate_pieces[:D_MODEL // piece], axis=1)
    gate_b_raw = jnp.concatenate(gate_pieces[D_MODEL // piece:], axis=1)

    t_chunk = lax.broadcasted_iota(jnp.int32, (SGU_BLOCK, SGU_BLOCK), 0) // CHUNK
    s_chunk = lax.broadcasted_iota(jnp.int32, (SGU_BLOCK, SGU_BLOCK), 1) // CHUNK
    causal = t_chunk >= s_chunk
    ws = [jnp.where(causal, ws_ref[g], 0.0).astype(BF16) for g in range(SGU_GROUPS)]
    blocks = []
    for n in range(tile // SGU_BLOCK):
        rows = slice(n * SGU_BLOCK, (n + 1) * SGU_BLOCK)
        cols = [
            _dot(ws[g], vln[rows, g * SGU_GROUP_DIM:(g + 1) * SGU_GROUP_DIM])
            for g in range(SGU_GROUPS)
        ]
        blocks.append(jnp.concatenate(cols, axis=1) + bs_ref[...])
    mixed = jnp.concatenate(blocks, axis=0)

    ya_tok = jnp.concatenate(
        [jnp.swapaxes(blk.reshape(steps, SUBLANES, GATE_BLOCK), 0, 1)
         .reshape(tile, GATE_BLOCK).astype(BF16) for blk in ya_blocks], axis=1)
    ya = _dot(ya_tok, wa_ref[...])
    yb = _dot((_gelu(u_raw) * mixed).astype(BF16), wb_ref[...])

    merged = (jax.nn.sigmoid(gate_a_raw) * ya + jax.nn.sigmoid(gate_b_raw) * yb).astype(BF16)
    o_ref[...] = x + _dot(merged, wo_ref[...])


def _ffn_kernel(x_ref, ng_ref, wu_ref, wd_ref, fg_ref, o_ref, h_ref, *, final_norm):
    x = x_ref[...]
    h_ref[...] = _rmsnorm(x, ng_ref[...]).astype(BF16)
    acc = x
    for c in range(D_FF // FF_CHUNK):
        lo = c * FF_CHUNK
        f = jnp.maximum(_dot(h_ref[...], wu_ref[:, lo:lo + FF_CHUNK]), 0.0)
        acc = acc + _dot((f * f).astype(BF16), wd_ref[lo:lo + FF_CHUNK, :])
    if final_norm:
        acc = _rmsnorm(acc, fg_ref[...])
    o_ref[...] = acc


class _Layer:
    def __init__(self, stacked, layer):
        self.stacked = stacked
        self.layer = layer


def _resident_spec(p):
    if isinstance(p, _Layer):
        shape = p.stacked.shape[1:]
        layer = p.layer
        return pl.BlockSpec((None,) + shape, lambda i: (layer,) + (0,) * len(shape),
                            pipeline_mode=pl.Buffered(1))
    ndim = p.ndim
    return pl.BlockSpec(p.shape, lambda i: (0,) * ndim, pipeline_mode=pl.Buffered(1))


def _operand(p):
    return p.stacked if isinstance(p, _Layer) else p


def _mixer_call(x2, params, *, seq_len):
    n_tok = x2.shape[0]
    tile = MIXER_TILE
    kern = functools.partial(_mixer_kernel, tile=tile, tiles_per_seq=seq_len // tile)
    tok_spec = pl.BlockSpec((tile, D_MODEL), lambda i: (i, 0))
    return pl.pallas_call(
        kern,
        grid=(n_tok // tile,),
        in_specs=[tok_spec] + [_resident_spec(p) for p in params],
        out_specs=tok_spec,
        out_shape=jax.ShapeDtypeStruct((n_tok, D_MODEL), F32),
        scratch_shapes=[
            pltpu.VMEM((tile, D_MODEL), BF16),
            pltpu.VMEM((tile, D_MODEL), BF16),
            pltpu.VMEM((CONV_TAIL, D_RNN), F32),
            pltpu.VMEM((SUBLANES, D_RNN), F32),
        ],
        compiler_params=pltpu.CompilerParams(
            dimension_semantics=("arbitrary",), vmem_limit_bytes=VMEM_LIMIT_BYTES),
        name="mixer",
    )(x2, *[_operand(p) for p in params])


def _ffn_call(x2, params, *, final_norm):
    n_tok = x2.shape[0]
    tile = FFN_TILE
    kern = functools.partial(_ffn_kernel, final_norm=final_norm)
    tok_spec = pl.BlockSpec((tile, D_MODEL), lambda i: (i, 0))
    return pl.pallas_call(
        kern,
        grid=(n_tok // tile,),
        in_specs=[tok_spec] + [_resident_spec(p) for p in params],
        out_specs=tok_spec,
        out_shape=jax.ShapeDtypeStruct((n_tok, D_MODEL), F32),
        scratch_shapes=[pltpu.VMEM((tile, D_MODEL), BF16)],
        compiler_params=pltpu.CompilerParams(
            dimension_semantics=("arbitrary",), vmem_limit_bytes=VMEM_LIMIT_BYTES),
        name="ffn",
    )(x2, *[_operand(p) for p in params])


def _gate_weights(w_a, w_x, b_a, b_x):
    eye = jnp.eye(HEADS_PER_GATE_BLOCK, dtype=w_a.dtype)

    def block_diag(w):
        w = w.reshape(N_GATE_BLOCKS, HEADS_PER_GATE_BLOCK, RNN_HEAD_DIM, RNN_HEAD_DIM)
        return jnp.einsum('jhik,hg->jhigk', w, eye).reshape(N_GATE_BLOCKS, GATE_BLOCK, GATE_BLOCK)

    wg = jnp.concatenate([block_diag(w_a), block_diag(w_x)], axis=-1).astype(BF16)
    bg = jnp.concatenate([b_a.reshape(N_GATE_BLOCKS, 1, GATE_BLOCK),
                          b_x.reshape(N_GATE_BLOCKS, 1, GATE_BLOCK)], axis=-1)
    return wg, bg


def kernel(x, norm_mix_g, w_in, conv_w, conv_b, lru_w_a, lru_b_a, lru_w_x, lru_b_x, lru_lambda, sgu_ln_g, sgu_ln_b, sgu_w_s, sgu_b_s, w_branch_a, w_branch_b, w_out, norm_ffn_g, w_up, w_down, final_norm_g):
    batch, seq_len, _ = x.shape
    depth = w_in.shape[0]
    assert seq_len % MIXER_TILE == 0 and (batch * seq_len) % FFN_TILE == 0
    x2 = x.reshape(batch * seq_len, D_MODEL)
    w_in, w_branch_a, w_branch_b, w_out, w_up, w_down = (
        w.astype(BF16) for w in (w_in, w_branch_a, w_branch_b, w_out, w_up, w_down))
    for l in range(depth):
        wg, bg = _gate_weights(lru_w_a[l], lru_w_x[l], lru_b_a[l], lru_b_x[l])
        bs = jnp.repeat(sgu_b_s[l].T, SGU_GROUP_DIM, axis=1)
        mixer_params = (
            norm_mix_g[l].reshape(1, D_MODEL),
            _Layer(w_in, l),
            conv_w[l],
            conv_b[l].reshape(1, D_RNN),
            wg, bg,
            lru_lambda[l].reshape(1, D_RNN),
            sgu_ln_g[l].reshape(1, D_SGU),
            sgu_ln_b[l].reshape(1, D_SGU),
            sgu_w_s[l],
            bs,
            _Layer(w_branch_a, l),
            _Layer(w_branch_b, l),
            _Layer(w_out, l),
        )
        x2 = _mixer_call(x2, mixer_params, seq_len=seq_len)
        ffn_params = (
            norm_ffn_g[l].reshape(1, D_MODEL),
            _Layer(w_up, l),
            _Layer(w_down, l),
            final_norm_g.reshape(1, D_MODEL),
        )
        x2 = _ffn_call(x2, ffn_params, final_norm=(l == depth - 1))
    return x2.reshape(batch, seq_len, D_MODEL)
```

```python
import functools

import jax
import jax.numpy as jnp
from jax import lax
from jax.experimental import pallas as pl
from jax.experimental.pallas import tpu as pltpu

D_MODEL = 1024
D_RNN = 1280
RNN_HEADS = 20
RNN_HEAD_DIM = D_RNN // RNN_HEADS
CONV_WIDTH = 4
LRU_C = 8.0
D_SGU = 1024
SGU_GROUPS = 8
SGU_GROUP_DIM = D_SGU // SGU_GROUPS
SGU_BLOCK = 128
CHUNK = 64
D_FF = 4 * D_MODEL
EPS = 1e-6

OFF_XRNN = 0
OFF_GRNN = OFF_XRNN + D_RNN
OFF_U = OFF_GRNN + D_RNN
OFF_V = OFF_U + D_SGU
OFF_GATE_A = OFF_V + D_SGU
OFF_GATE_B = OFF_GATE_A + D_MODEL
D_IN = OFF_GATE_B + D_MODEL

SUBLANES = 8
GATE_BLOCK = 256
N_GATE_BLOCKS = D_RNN // GATE_BLOCK
HEADS_PER_GATE_BLOCK = GATE_BLOCK // RNN_HEAD_DIM
CONV_TAIL = (CONV_WIDTH - 1) * SUBLANES
FF_CHUNK = 1024

MIXER_TILE = 256
FFN_TILE = 1024
V7X_VMEM_BYTES = 64 * 1024 * 1024
VMEM_LIMIT_BYTES = V7X_VMEM_BYTES * 7 // 8

F32 = jnp.float32
BF16 = jnp.bfloat16


def _dot(a, b):
    return jnp.dot(a, b, preferred_element_type=F32)


def _rmsnorm(x, g):
    return x * lax.rsqrt(jnp.mean(x * x, axis=-1, keepdims=True) + EPS) * g


SQRT_2_OVER_PI = 0.7978845608028654
LOG2_E = 1.4426950408889634
GELU_CUBIC = 0.044715


def _gelu(x):
    c = -2.0 * SQRT_2_OVER_PI * LOG2_E
    return x / (1.0 + jnp.exp2(x * (c + (c * GELU_CUBIC) * (x * x))))


def _sqrt_nonneg(y):
    return jnp.where(y > 0.0, y * lax.rsqrt(y), 0.0)


def _shift_chains(v):
    return pltpu.roll(v, 1, 0)


def _mixer_kernel(x_ref, ng_ref, win_ref, cw_ref, cb_ref, wg_ref, bg_ref,
                  lam_ref, lng_ref, lnb_ref, ws_ref, bs_ref, wa_ref, wb_ref, wo_ref,
                  o_ref, h_ref, hp_ref, tail_ref, hc_ref, *, tile, tiles_per_seq):
    steps = tile // SUBLANES
    seq_tile = pl.program_id(0) % tiles_per_seq

    @pl.when(seq_tile == 0)
    def _():
        tail_ref[...] = jnp.zeros((CONV_TAIL, D_RNN), F32)
        hc_ref[...] = jnp.zeros((SUBLANES, D_RNN), F32)

    x = x_ref[...]
    hn = _rmsnorm(x, ng_ref[...])
    h_ref[...] = hn.astype(BF16)
    hp_ref[...] = jnp.swapaxes(
        hn.reshape(SUBLANES, steps, D_MODEL), 0, 1).reshape(tile, D_MODEL).astype(BF16)

    def proj(lo, width):
        return _dot(h_ref[...], win_ref[:, lo:lo + width])

    def proj_tm(lo, width):
        return _dot(hp_ref[...], win_ref[:, lo:lo + width])

    x_rnn = proj_tm(OFF_XRNN, D_RNN)
    v_raw = proj(OFF_V, D_SGU)

    tail = x_rnn[tile - CONV_TAIL:, :]
    row = lax.broadcasted_iota(jnp.int32, (SUBLANES, D_RNN), 0)
    prev_tail = tail_ref[...]
    head = [
        _shift_chains(jnp.where(row == SUBLANES - 1,
                                prev_tail[q * SUBLANES:(q + 1) * SUBLANES, :],
                                tail[q * SUBLANES:(q + 1) * SUBLANES, :]))
        for q in range(CONV_WIDTH - 1)
    ]
    tail_ref[...] = tail
    x_ext = jnp.concatenate(head + [x_rnn], axis=0)
    xr = cb_ref[...]
    for k in range(CONV_WIDTH):
        xr = xr + x_ext[k * SUBLANES:k * SUBLANES + tile, :] * cw_ref[k:k + 1, :]

    gv = _gelu(v_raw)
    mu = jnp.mean(gv, axis=-1, keepdims=True)
    vc = gv - mu
    var = jnp.mean(vc * vc, axis=-1, keepdims=True)
    vln = (vc * lax.rsqrt(var + EPS) * lng_ref[...] + lnb_ref[...]).astype(BF16)
    u_raw = proj(OFF_U, D_SGU)

    piece = GATE_BLOCK
    gate_offsets = [OFF_GATE_A + c for c in range(0, 2 * D_MODEL, piece)]
    per_block = -(-len(gate_offsets) // N_GATE_BLOCKS)
    gate_pieces = []

    neg_lam = -lam_ref[...]
    softplus = jnp.maximum(neg_lam, 0.0) + jnp.log1p(jnp.exp(-jnp.abs(neg_lam)))
    decay_rate = LRU_C * softplus
    row_b = lax.broadcasted_iota(jnp.int32, (SUBLANES, GATE_BLOCK), 0)
    ya_blocks = []
    for blk in range(N_GATE_BLOCKS):
        lo = blk * GATE_BLOCK
        xr_b = xr[:, lo:lo + GATE_BLOCK]
        z = _dot(xr_b.astype(BF16), wg_ref[blk]) + bg_ref[blk]
        g_rnn = proj_tm(OFF_GRNN + lo, GATE_BLOCK)
        for off in gate_offsets[blk * per_block:(blk + 1) * per_block]:
            gate_pieces.append(proj(off, piece))
        r = jax.nn.sigmoid(z[:, :GATE_BLOCK])
        i = jax.nn.sigmoid(z[:, GATE_BLOCK:])
        neg_log_a = r * decay_rate[:, lo:lo + GATE_BLOCK]
        a = jnp.exp(-neg_log_a)
        norm = _sqrt_nonneg(jnp.tanh(neg_log_a) * (1.0 + a * a))
        u = norm * (i * xr_b)

        hl = u[0:SUBLANES, :]
        pr = a[0:SUBLANES, :]
        local_states = [hl]
        decays = [pr]
        for s in range(1, steps):
            a_s = a[s * SUBLANES:(s + 1) * SUBLANES, :]
            hl = a_s * hl + u[s * SUBLANES:(s + 1) * SUBLANES, :]
            pr = pr * a_s
            local_states.append(hl)
            decays.append(pr)
        for k in (1, 2, 4):
            keep = row_b >= k
            p_prev = jnp.where(keep, pltpu.roll(pr, k, 0), 1.0)
            h_prev = jnp.where(keep, pltpu.roll(hl, k, 0), 0.0)
            hl = pr * h_prev + hl
            pr = pr * p_prev
        carry = hc_ref[:, lo:lo + GATE_BLOCK]
        ends = hl + pr * carry
        entry = jnp.where(row_b >= 1, _shift_chains(ends), carry)
        hc_ref[:, lo:lo + GATE_BLOCK] = jnp.broadcast_to(
            ends[SUBLANES - 1:SUBLANES, :], (SUBLANES, GATE_BLOCK))
        states = (jnp.concatenate(local_states, axis=0)
                  + jnp.concatenate(decays, axis=0) * jnp.concatenate([entry] * steps, axis=0))
        ya_blocks.append(states * _gelu(g_rnn))
    gate_a_raw = jnp.concatenate(gate_pieces[:D_MODEL // piece], axis=1)
    gate_b_raw = jnp.concatenate(gate_pieces[D_MODEL // piece:], axis=1)

    t_chunk = lax.broadcasted_iota(jnp.int32, (SGU_BLOCK, SGU_BLOCK), 0) // CHUNK
    s_chunk = lax.broadcasted_iota(jnp.int32, (SGU_BLOCK, SGU_BLOCK), 1) // CHUNK
    causal = t_chunk >= s_chunk
    ws = [jnp.where(causal, ws_ref[g], 0.0).astype(BF16) for g in range(SGU_GROUPS)]
    blocks = []
    for n in range(tile // SGU_BLOCK):
        rows = slice(n * SGU_BLOCK, (n + 1) * SGU_BLOCK)
        cols = [
            _dot(ws[g], vln[rows, g * SGU_GROUP_DIM:(g + 1) * SGU_GROUP_DIM])
            for g in range(SGU_GROUPS)
        ]
        blocks.append(jnp.concatenate(cols, axis=1) + bs_ref[...])
    mixed = jnp.concatenate(blocks, axis=0)

    ya_tok = jnp.concatenate(
        [jnp.swapaxes(blk.reshape(steps, SUBLANES, GATE_BLOCK), 0, 1)
         .reshape(tile, GATE_BLOCK).astype(BF16) for blk in ya_blocks], axis=1)
    ya = _dot(ya_tok, wa_ref[...])
    yb = _dot((_gelu(u_raw) * mixed).astype(BF16), wb_ref[...])

    merged = (jax.nn.sigmoid(gate_a_raw) * ya + jax.nn.sigmoid(gate_b_raw) * yb).astype(BF16)
    o_ref[...] = x + _dot(merged, wo_ref[...])


def _ffn_kernel(x_ref, ng_ref, wu_ref, wd_ref, fg_ref, o_ref, h_ref, *, final_norm):
    x = x_ref[...]
    h_ref[...] = _rmsnorm(x, ng_ref[...]).astype(BF16)
    acc = x
    for c in range(D_FF // FF_CHUNK):
        lo = c * FF_CHUNK
        f = jnp.maximum(_dot(h_ref[...], wu_ref[:, lo:lo + FF_CHUNK]), 0.0)
        acc = acc + _dot((f * f).astype(BF16), wd_ref[lo:lo + FF_CHUNK, :])
    if final_norm:
        acc = _rmsnorm(acc, fg_ref[...])
    o_ref[...] = acc


class _Layer:
    def __init__(self, stacked, layer):
        self.stacked = stacked
        self.layer = layer


def _resident_spec(p):
    if isinstance(p, _Layer):
        shape = p.stacked.shape[1:]
        layer = p.layer
        return pl.BlockSpec((None,) + shape, lambda i: (layer,) + (0,) * len(shape),
                            pipeline_mode=pl.Buffered(1))
    ndim = p.ndim
    return pl.BlockSpec(p.shape, lambda i: (0,) * ndim, pipeline_mode=pl.Buffered(1))


def _operand(p):
    return p.stacked if isinstance(p, _Layer) else p


def _mixer_call(x2, params, *, seq_len):
    n_tok = x2.shape[0]
    tile = MIXER_TILE
    kern = functools.partial(_mixer_kernel, tile=tile, tiles_per_seq=seq_len // tile)
    tok_spec = pl.BlockSpec((tile, D_MODEL), lambda i: (i, 0))
    return pl.pallas_call(
        kern,
        grid=(n_tok // tile,),
        in_specs=[tok_spec] + [_resident_spec(p) for p in params],
        out_specs=tok_spec,
        out_shape=jax.ShapeDtypeStruct((n_tok, D_MODEL), F32),
        scratch_shapes=[
            pltpu.VMEM((tile, D_MODEL), BF16),
            pltpu.VMEM((tile, D_MODEL), BF16),
            pltpu.VMEM((CONV_TAIL, D_RNN), F32),
            pltpu.VMEM((SUBLANES, D_RNN), F32),
        ],
        compiler_params=pltpu.CompilerParams(
            dimension_semantics=("arbitrary",), vmem_limit_bytes=VMEM_LIMIT_BYTES),
        name="mixer",
    )(x2, *[_operand(p) for p in params])


def _ffn_call(x2, params, *, final_norm):
    n_tok = x2.shape[0]
    tile = FFN_TILE
    kern = functools.partial(_ffn_kernel, final_norm=final_norm)
    tok_spec = pl.BlockSpec((tile, D_MODEL), lambda i: (i, 0))
    return pl.pallas_call(
        kern,
        grid=(n_tok // tile,),
        in_specs=[tok_spec] + [_resident_spec(p) for p in params],
        out_specs=tok_spec,
        out_shape=jax.ShapeDtypeStruct((n_tok, D_MODEL), F32),
        scratch_shapes=[pltpu.VMEM((tile, D_MODEL), BF16)],
        compiler_params=pltpu.CompilerParams(
            dimension_semantics=("arbitrary",), vmem_limit_bytes=VMEM_LIMIT_BYTES),
        name="ffn",
    )(x2, *[_operand(p) for p in params])


def _gate_weights(w_a, w_x, b_a, b_x):
    eye = jnp.eye(HEADS_PER_GATE_BLOCK, dtype=w_a.dtype)

    def block_diag(w):
        w = w.reshape(N_GATE_BLOCKS, HEADS_PER_GATE_BLOCK, RNN_HEAD_DIM, RNN_HEAD_DIM)
        return jnp.einsum('jhik,hg->jhigk', w, eye).reshape(N_GATE_BLOCKS, GATE_BLOCK, GATE_BLOCK)

    wg = jnp.concatenate([block_diag(w_a), block_diag(w_x)], axis=-1).astype(BF16)
    bg = jnp.concatenate([b_a.reshape(N_GATE_BLOCKS, 1, GATE_BLOCK),
                          b_x.reshape(N_GATE_BLOCKS, 1, GATE_BLOCK)], axis=-1)
    return wg, bg


def kernel(x, norm_mix_g, w_in, conv_w, conv_b, lru_w_a, lru_b_a, lru_w_x, lru_b_x, lru_lambda, sgu_ln_g, sgu_ln_b, sgu_w_s, sgu_b_s, w_branch_a, w_branch_b, w_out, norm_ffn_g, w_up, w_down, final_norm_g):
    batch, seq_len, _ = x.shape
    depth = w_in.shape[0]
    assert seq_len % MIXER_TILE == 0 and (batch * seq_len) % FFN_TILE == 0
    x2 = x.reshape(batch * seq_len, D_MODEL)
    w_in, w_branch_a, w_branch_b, w_out, w_up, w_down = (
        w.astype(BF16) for w in (w_in, w_branch_a, w_branch_b, w_out, w_up, w_down))
    for l in range(depth):
        wg, bg = _gate_weights(lru_w_a[l], lru_w_x[l], lru_b_a[l], lru_b_x[l])
        bs = jnp.repeat(sgu_b_s[l].T, SGU_GROUP_DIM, axis=1)
        mixer_params = (
            norm_mix_g[l].reshape(1, D_MODEL),
            _Layer(w_in, l),
            conv_w[l],
            conv_b[l].reshape(1, D_RNN),
            wg, bg,
            lru_lambda[l].reshape(1, D_RNN),
            sgu_ln_g[l].reshape(1, D_SGU),
            sgu_ln_b[l].reshape(1, D_SGU),
            sgu_w_s[l],
            bs,
            _Layer(w_branch_a, l),
            _Layer(w_branch_b, l),
            _Layer(w_out, l),
        )
        x2 = _mixer_call(x2, mixer_params, seq_len=seq_len)
        ffn_params = (
            norm_ffn_g[l].reshape(1, D_MODEL),
            _Layer(w_up, l),
            _Layer(w_down, l),
            final_norm_g.reshape(1, D_MODEL),
        )
        x2 = _ffn_call(x2, ffn_params, final_norm=(l == depth - 1))
    return x2.reshape(batch, seq_len, D_MODEL)
```

```python
import functools

import jax
import jax.numpy as jnp
from jax import lax
from jax.experimental import pallas as pl
from jax.experimental.pallas import tpu as pltpu

D_MODEL = 1024
D_RNN = 1280
RNN_HEADS = 20
RNN_HEAD_DIM = D_RNN // RNN_HEADS
CONV_WIDTH = 4
LRU_C = 8.0
D_SGU = 1024
SGU_GROUPS = 8
SGU_GROUP_DIM = D_SGU // SGU_GROUPS
SGU_BLOCK = 128
CHUNK = 64
D_FF = 4 * D_MODEL
EPS = 1e-6

OFF_XRNN = 0
OFF_GRNN = OFF_XRNN + D_RNN
OFF_U = OFF_GRNN + D_RNN
OFF_V = OFF_U + D_SGU
OFF_GATE_A = OFF_V + D_SGU
OFF_GATE_B = OFF_GATE_A + D_MODEL
D_IN = OFF_GATE_B + D_MODEL

SUBLANES = 8
LANES = 128
GATE_BLOCK = 256
N_GATE_BLOCKS = D_RNN // GATE_BLOCK
HEADS_PER_GATE_BLOCK = GATE_BLOCK // RNN_HEAD_DIM
CONV_TAIL = (CONV_WIDTH - 1) * SUBLANES
FF_CHUNK = 1024

MIXER_TILE = 256
FFN_TILE = 1024
V7X_VMEM_BYTES = 64 * 1024 * 1024
VMEM_LIMIT_BYTES = V7X_VMEM_BYTES * 7 // 8

F32 = jnp.float32
BF16 = jnp.bfloat16


def _dot(a, b):
    return jnp.dot(a, b, preferred_element_type=F32)


def _rmsnorm(x, g):
    return x * lax.rsqrt(jnp.mean(x * x, axis=-1, keepdims=True) + EPS) * g


SQRT_2_OVER_PI = 0.7978845608028654
LOG2_E = 1.4426950408889634
GELU_CUBIC = 0.044715


def _gelu(x):
    c = -2.0 * SQRT_2_OVER_PI * LOG2_E
    return x / (1.0 + jnp.exp2(x * (c + (c * GELU_CUBIC) * (x * x))))


def _sqrt_nonneg(y):
    return jnp.where(y > 0.0, y * lax.rsqrt(y), 0.0)


def _shift_chains(v):
    return pltpu.roll(v, 1, 0)


def _mixer_kernel(x_ref, ng_ref, win_ref, cw_ref, cb_ref, wg_ref, bg_ref,
                  lam_ref, lng_ref, lnb_ref, ws_ref, bs_ref, wa_ref, wb_ref, wo_ref,
                  o_ref, h_ref, hp_ref, tail_ref, hc_ref, *, tile, tiles_per_seq):
    steps = tile // SUBLANES
    seq_tile = pl.program_id(0) % tiles_per_seq

    @pl.when(seq_tile == 0)
    def _():
        tail_ref[...] = jnp.zeros((CONV_TAIL, D_RNN), F32)
        hc_ref[...] = jnp.zeros((SUBLANES, D_RNN), F32)

    x = x_ref[...]
    hn = _rmsnorm(x, ng_ref[...])
    h_ref[...] = hn.astype(BF16)
    hp_ref[...] = jnp.swapaxes(
        hn.reshape(SUBLANES, steps, D_MODEL), 0, 1).reshape(tile, D_MODEL).astype(BF16)

    def proj(lo, width):
        return _dot(h_ref[...], win_ref[:, lo:lo + width])

    def proj_tm(lo, width):
        return _dot(hp_ref[...], win_ref[:, lo:lo + width])

    x_rnn = proj_tm(OFF_XRNN, D_RNN)
    v_raw = proj(OFF_V, D_SGU)

    tail = x_rnn[tile - CONV_TAIL:, :]
    row = lax.broadcasted_iota(jnp.int32, (SUBLANES, D_RNN), 0)
    prev_tail = tail_ref[...]
    head = [
        _shift_chains(jnp.where(row == SUBLANES - 1,
                                prev_tail[q * SUBLANES:(q + 1) * SUBLANES, :],
                                tail[q * SUBLANES:(q + 1) * SUBLANES, :]))
        for q in range(CONV_WIDTH - 1)
    ]
    tail_ref[...] = tail
    x_ext = jnp.concatenate(head + [x_rnn], axis=0)
    xr = cb_ref[...]
    for k in range(CONV_WIDTH):
        xr = xr + x_ext[k * SUBLANES:k * SUBLANES + tile, :] * cw_ref[k:k + 1, :]

    gv = _gelu(v_raw)
    mu = jnp.mean(gv, axis=-1, keepdims=True)
    vc = gv - mu
    var = jnp.mean(vc * vc, axis=-1, keepdims=True)
    vln = (vc * lax.rsqrt(var + EPS) * lng_ref[...] + lnb_ref[...]).astype(BF16)
    u_raw = proj(OFF_U, D_SGU)

    piece = GATE_BLOCK
    gate_offsets = [OFF_GATE_A + c for c in range(0, 2 * D_MODEL, piece)]
    per_block = -(-len(gate_offsets) // N_GATE_BLOCKS)
    gate_pieces = []

    neg_lam = -lam_ref[...]
    softplus = jnp.maximum(neg_lam, 0.0) + jnp.log1p(jnp.exp(-jnp.abs(neg_lam)))
    decay_rate = LRU_C * softplus
    row_b = lax.broadcasted_iota(jnp.int32, (SUBLANES, LANES), 0)
    ya_blocks = []
    for blk in range(N_GATE_BLOCKS):
        lo = blk * GATE_BLOCK
        xr_b = xr[:, lo:lo + GATE_BLOCK]
        z = _dot(xr_b.astype(BF16), wg_ref[blk]) + bg_ref[blk]
        g_rnn = proj_tm(OFF_GRNN + lo, GATE_BLOCK)
        for off in gate_offsets[blk * per_block:(blk + 1) * per_block]:
            gate_pieces.append(proj(off, piece))
        r = jax.nn.sigmoid(z[:, :GATE_BLOCK])
        i = jax.nn.sigmoid(z[:, GATE_BLOCK:])
        neg_log_a = r * decay_rate[:, lo:lo + GATE_BLOCK]
        a = jnp.exp(-neg_log_a)
        norm = _sqrt_nonneg(jnp.tanh(neg_log_a) * (1.0 + a * a))
        u = norm * (i * xr_b)

        state_cols = []
        for half in range(GATE_BLOCK // LANES):
            cols = slice(half * LANES, (half + 1) * LANES)
            a_steps = [a[s * SUBLANES:(s + 1) * SUBLANES, cols] for s in range(steps)]
            u_steps = [u[s * SUBLANES:(s + 1) * SUBLANES, cols] for s in range(steps)]
            hl = u_steps[0]
            pr = a_steps[0]
            for s in range(1, steps):
                hl = a_steps[s] * hl + u_steps[s]
                pr = pr * a_steps[s]
            for k in (1, 2, 4):
                keep = row_b >= k
                p_prev = jnp.where(keep, pltpu.roll(pr, k, 0), 1.0)
                h_prev = jnp.where(keep, pltpu.roll(hl, k, 0), 0.0)
                hl = pr * h_prev + hl
                pr = pr * p_prev
            c_lo = lo + half * LANES
            carry = hc_ref[:, c_lo:c_lo + LANES]
            ends = hl + pr * carry
            state = jnp.where(row_b >= 1, _shift_chains(ends), carry)
            hc_ref[:, c_lo:c_lo + LANES] = jnp.broadcast_to(
                ends[SUBLANES - 1:SUBLANES, :], (SUBLANES, LANES))
            rows = []
            for s in range(steps):
                state = a_steps[s] * state + u_steps[s]
                rows.append(state)
            state_cols.append(jnp.concatenate(rows, axis=0))
        states = jnp.concatenate(state_cols, axis=1)
        ya_blocks.append(states * _gelu(g_rnn))
    gate_a_raw = jnp.concatenate(gate_pieces[:D_MODEL // piece], axis=1)
    gate_b_raw = jnp.concatenate(gate_pieces[D_MODEL // piece:], axis=1)

    t_chunk = lax.broadcasted_iota(jnp.int32, (SGU_BLOCK, SGU_BLOCK), 0) // CHUNK
    s_chunk = lax.broadcasted_iota(jnp.int32, (SGU_BLOCK, SGU_BLOCK), 1) // CHUNK
    causal = t_chunk >= s_chunk
    ws = [jnp.where(causal, ws_ref[g], 0.0).astype(BF16) for g in range(SGU_GROUPS)]
    blocks = []
    for n in range(tile // SGU_BLOCK):
        rows = slice(n * SGU_BLOCK, (n + 1) * SGU_BLOCK)
        cols = [
            _dot(ws[g], vln[rows, g * SGU_GROUP_DIM:(g + 1) * SGU_GROUP_DIM])
            for g in range(SGU_GROUPS)
        ]
        blocks.append(jnp.concatenate(cols, axis=1) + bs_ref[...])
    mixed = jnp.concatenate(blocks, axis=0)

    ya_tok = jnp.concatenate(
        [jnp.swapaxes(blk.reshape(steps, SUBLANES, GATE_BLOCK), 0, 1)
         .reshape(tile, GATE_BLOCK).astype(BF16) for blk in ya_blocks], axis=1)
    ya = _dot(ya_tok, wa_ref[...])
    yb = _dot((_gelu(u_raw) * mixed).astype(BF16), wb_ref[...])

    merged = (jax.nn.sigmoid(gate_a_raw) * ya + jax.nn.sigmoid(gate_b_raw) * yb).astype(BF16)
    o_ref[...] = x + _dot(merged, wo_ref[...])


def _ffn_kernel(x_ref, ng_ref, wu_ref, wd_ref, fg_ref, o_ref, h_ref, *, final_norm):
    x = x_ref[...]
    h_ref[...] = _rmsnorm(x, ng_ref[...]).astype(BF16)
    acc = x
    for c in range(D_FF // FF_CHUNK):
        lo = c * FF_CHUNK
        f = jnp.maximum(_dot(h_ref[...], wu_ref[:, lo:lo + FF_CHUNK]), 0.0)
        acc = acc + _dot((f * f).astype(BF16), wd_ref[lo:lo + FF_CHUNK, :])
    if final_norm:
        acc = _rmsnorm(acc, fg_ref[...])
    o_ref[...] = acc


class _Layer:
    def __init__(self, stacked, layer):
        self.stacked = stacked
        self.layer = layer


def _resident_spec(p):
    if isinstance(p, _Layer):
        shape = p.stacked.shape[1:]
        layer = p.layer
        return pl.BlockSpec((None,) + shape, lambda i: (layer,) + (0,) * len(shape),
                            pipeline_mode=pl.Buffered(1))
    ndim = p.ndim
    return pl.BlockSpec(p.shape, lambda i: (0,) * ndim, pipeline_mode=pl.Buffered(1))


def _operand(p):
    return p.stacked if isinstance(p, _Layer) else p


def _mixer_call(x2, params, *, seq_len):
    n_tok = x2.shape[0]
    tile = MIXER_TILE
    kern = functools.partial(_mixer_kernel, tile=tile, tiles_per_seq=seq_len // tile)
    tok_spec = pl.BlockSpec((tile, D_MODEL), lambda i: (i, 0))
    return pl.pallas_call(
        kern,
        grid=(n_tok // tile,),
        in_specs=[tok_spec] + [_resident_spec(p) for p in params],
        out_specs=tok_spec,
        out_shape=jax.ShapeDtypeStruct((n_tok, D_MODEL), F32),
        scratch_shapes=[
            pltpu.VMEM((tile, D_MODEL), BF16),
            pltpu.VMEM((tile, D_MODEL), BF16),
            pltpu.VMEM((CONV_TAIL, D_RNN), F32),
            pltpu.VMEM((SUBLANES, D_RNN), F32),
        ],
        compiler_params=pltpu.CompilerParams(
            dimension_semantics=("arbitrary",), vmem_limit_bytes=VMEM_LIMIT_BYTES),
        name="mixer",
    )(x2, *[_operand(p) for p in params])


def _ffn_call(x2, params, *, final_norm):
    n_tok = x2.shape[0]
    tile = FFN_TILE
    kern = functools.partial(_ffn_kernel, final_norm=final_norm)
    tok_spec = pl.BlockSpec((tile, D_MODEL), lambda i: (i, 0))
    return pl.pallas_call(
        kern,
        grid=(n_tok // tile,),
        in_specs=[tok_spec] + [_resident_spec(p) for p in params],
        out_specs=tok_spec,
        out_shape=jax.ShapeDtypeStruct((n_tok, D_MODEL), F32),
        scratch_shapes=[pltpu.VMEM((tile, D_MODEL), BF16)],
        compiler_params=pltpu.CompilerParams(
            dimension_semantics=("arbitrary",), vmem_limit_bytes=VMEM_LIMIT_BYTES),
        name="ffn",
    )(x2, *[_operand(p) for p in params])


def _gate_weights(w_a, w_x, b_a, b_x):
    eye = jnp.eye(HEADS_PER_GATE_BLOCK, dtype=w_a.dtype)

    def block_diag(w):
        w = w.reshape(N_GATE_BLOCKS, HEADS_PER_GATE_BLOCK, RNN_HEAD_DIM, RNN_HEAD_DIM)
        return jnp.einsum('jhik,hg->jhigk', w, eye).reshape(N_GATE_BLOCKS, GATE_BLOCK, GATE_BLOCK)

    wg = jnp.concatenate([block_diag(w_a), block_diag(w_x)], axis=-1).astype(BF16)
    bg = jnp.concatenate([b_a.reshape(N_GATE_BLOCKS, 1, GATE_BLOCK),
                          b_x.reshape(N_GATE_BLOCKS, 1, GATE_BLOCK)], axis=-1)
    return wg, bg


def kernel(x, norm_mix_g, w_in, conv_w, conv_b, lru_w_a, lru_b_a, lru_w_x, lru_b_x, lru_lambda, sgu_ln_g, sgu_ln_b, sgu_w_s, sgu_b_s, w_branch_a, w_branch_b, w_out, norm_ffn_g, w_up, w_down, final_norm_g):
    batch, seq_len, _ = x.shape
    depth = w_in.shape[0]
    assert seq_len % MIXER_TILE == 0 and (batch * seq_len) % FFN_TILE == 0
    x2 = x.reshape(batch * seq_len, D_MODEL)
    w_in, w_branch_a, w_branch_b, w_out, w_up, w_down = (
        w.astype(BF16) for w in (w_in, w_branch_a, w_branch_b, w_out, w_up, w_down))
    for l in range(depth):
        wg, bg = _gate_weights(lru_w_a[l], lru_w_x[l], lru_b_a[l], lru_b_x[l])
        bs = jnp.repeat(sgu_b_s[l].T, SGU_GROUP_DIM, axis=1)
        mixer_params = (
            norm_mix_g[l].reshape(1, D_MODEL),
            _Layer(w_in, l),
            conv_w[l],
            conv_b[l].reshape(1, D_RNN),
            wg, bg,
            lru_lambda[l].reshape(1, D_RNN),
            sgu_ln_g[l].reshape(1, D_SGU),
            sgu_ln_b[l].reshape(1, D_SGU),
            sgu_w_s[l],
            bs,
            _Layer(w_branch_a, l),
            _Layer(w_branch_b, l),
            _Layer(w_out, l),
        )
        x2 = _mixer_call(x2, mixer_params, seq_len=seq_len)
        ffn_params = (
            norm_ffn_g[l].reshape(1, D_MODEL),
            _Layer(w_up, l),
            _Layer(w_down, l),
            final_norm_g.reshape(1, D_MODEL),
        )
        x2 = _ffn_call(x2, ffn_params, final_norm=(l == depth - 1))
    return x2.reshape(batch, seq_len, D_MODEL)
```
